```python
import jax, jax.numpy as jnp
from jax import lax
import numpy as np

D_MODEL = 2048
BATCH = 4
SEQ = 4096
DEPTH = 4

CHUNK = 64
N_A_LAYERS = DEPTH // 2
N_B_LAYERS = DEPTH - N_A_LAYERS
RMS_EPS = 1e-6
POOL_WINDOWS = (2, 4, 8, 16)
N_POOL_GROUPS = len(POOL_WINDOWS)
POOL_GROUP = D_MODEL // N_POOL_GROUPS
MLA_HEADS = 16
Q_RANK = D_MODEL // 4
KV_RANK = D_MODEL // 4
NOPE_DIM = 128
ROPE_DIM = 64
V_DIM = 128
ROPE_THETA = 10000.0
Q_BLOCK = 128
MEM_TOKENS = 256
MEM_HEADS = 4
MEM_HEAD_DIM = 128
MEM_INNER = MEM_HEADS * MEM_HEAD_DIM
N_GROUPS = 4
EXPERTS_PER_GROUP = 8
N_EXPERTS = N_GROUPS * EXPERTS_PER_GROUP
TOP_K = 2
EXPERT_FF = D_MODEL // 4
MOE_BLOCK = 256

kernel_name = "yoco_pool_mla_hier_moe_trunk"


def rmsnorm(x, g):
    xf = x.astype(jnp.float32)
    y = xf * lax.rsqrt(jnp.mean(xf * xf, axis=-1, keepdims=True) + RMS_EPS)
    return (y * g.astype(jnp.float32)).astype(x.dtype)


def rope_tables(positions):
    half = ROPE_DIM // 2
    inv = 1.0 / (ROPE_THETA ** (jnp.arange(half, dtype=jnp.float32) / half))
    ang = positions.astype(jnp.float32)[..., None] * inv
    return jnp.cos(ang), jnp.sin(ang)


def apply_rope(x, cos, sin):
    if x.ndim == 4:
        cos, sin = cos[:, :, None, :], sin[:, :, None, :]
    half = ROPE_DIM // 2
    x1 = x[..., :half].astype(jnp.float32)
    x2 = x[..., half:].astype(jnp.float32)
    return jnp.concatenate([x1 * cos - x2 * sin, x2 * cos + x1 * sin], axis=-1).astype(x.dtype)


def pool_mixer(h, w_pool, scale):
    bsz, s, d = h.shape
    hf = h.astype(jnp.float32)
    cs = jnp.concatenate([jnp.zeros((bsz, 1, d), jnp.float32), jnp.cumsum(hf, axis=1)], axis=1)
    t = np.arange(s)
    means = []
    for g, w in enumerate(POOL_WINDOWS):
        sl = slice(g * POOL_GROUP, (g + 1) * POOL_GROUP)
        lo = np.maximum(t + 1 - w, 0)
        cnt = np.minimum(t + 1, w).astype(np.float32)
        means.append((cs[:, 1:, sl] - cs[:, lo, sl]) / cnt[None, :, None])
    u = (jnp.concatenate(means, axis=-1) - hf).astype(h.dtype)
    u = u.reshape(bsz, s, N_POOL_GROUPS, POOL_GROUP)
    y = jnp.einsum('bsgc,gcd->bsgd', u, w_pool).reshape(bsz, s, d)
    return y * scale


def mem_attention(h, m, wq, wk, wv, wo):
    bsz, s, _ = h.shape
    q = (h @ wq).reshape(bsz, s, MEM_HEADS, MEM_HEAD_DIM)
    k = (m @ wk).reshape(bsz, -1, MEM_HEADS, MEM_HEAD_DIM)
    v = (m @ wv).reshape(bsz, -1, MEM_HEADS, MEM_HEAD_DIM)
    sc = jnp.einsum('bshd,bmhd->bhsm', q, k).astype(jnp.float32) * (MEM_HEAD_DIM ** -0.5)
    p = jax.nn.softmax(sc, axis=-1).astype(v.dtype)
    o = jnp.einsum('bhsm,bmhd->bshd', p, v).reshape(bsz, s, MEM_INNER)
    return o @ wo


def mla_shared_kv(h, cos, sin, kv_in_norm, w_dkv, kv_latent_norm, w_kr, w_uk, w_uv):
    bsz, s, _ = h.shape
    hn = rmsnorm(h, kv_in_norm)
    c_kv = rmsnorm(hn @ w_dkv, kv_latent_norm)
    k_rope = apply_rope(hn @ w_kr, cos, sin)
    k_nope = (c_kv @ w_uk).reshape(bsz, s, MLA_HEADS, NOPE_DIM)
    v = (c_kv @ w_uv).reshape(bsz, s, MLA_HEADS, V_DIM)
    return k_nope, k_rope, v


def mla_attention(hn, cos, sin, k_nope, k_rope, v, w_dq, q_latent_norm, w_uq, w_qr, w_o):
    bsz, s, _ = hn.shape
    c_q = rmsnorm(hn @ w_dq, q_latent_norm)
    q_nope = (c_q @ w_uq).reshape(bsz, s, MLA_HEADS, NOPE_DIM)
    q_rope = apply_rope((c_q @ w_qr).reshape(bsz, s, MLA_HEADS, ROPE_DIM), cos, sin)
    nb = s // Q_BLOCK
    qn = q_nope.reshape(bsz, nb, Q_BLOCK, MLA_HEADS, NOPE_DIM).transpose(1, 0, 2, 3, 4)
    qr = q_rope.reshape(bsz, nb, Q_BLOCK, MLA_HEADS, ROPE_DIM).transpose(1, 0, 2, 3, 4)
    key_chunk = jnp.arange(s) // CHUNK
    scale = (NOPE_DIM + ROPE_DIM) ** -0.5

    def attend(args):
        qn_b, qr_b, b = args
        sc = (jnp.einsum('bqhd,bkhd->bhqk', qn_b, k_nope)
              + jnp.einsum('bqhr,bkr->bhqk', qr_b, k_rope)).astype(jnp.float32) * scale
        q_chunk = (b * Q_BLOCK + jnp.arange(Q_BLOCK)) // CHUNK
        allowed = key_chunk[None, :] <= q_chunk[:, None]
        sc = jnp.where(allowed[None, None], sc, -jnp.inf)
        p = jax.nn.softmax(sc, axis=-1).astype(v.dtype)
        return jnp.einsum('bhqk,bkhd->bqhd', p, v)

    o = lax.map(attend, (qn, qr, jnp.arange(nb)))
    o = o.transpose(1, 0, 2, 3, 4).reshape(bsz, s, MLA_HEADS * V_DIM)
    return o @ w_o


def hier_moe(h, rg_w, rg_b, re_w, re_b, w_gate, w_up, w_down):
    bsz, s, d = h.shape
    t_tok = bsz * s
    xt = h.reshape(t_tok, d)
    rows = jnp.arange(t_tok)
    g_logits = (xt @ rg_w + rg_b).astype(jnp.float32)
    g_sel = jnp.argmax(g_logits, axis=-1)
    g_p = jax.nn.softmax(g_logits, axis=-1)[rows, g_sel][:, None]
    e_logits = (xt @ re_w + re_b).astype(jnp.float32).reshape(t_tok, N_GROUPS, EXPERTS_PER_GROUP)
    e_p = jax.nn.softmax(e_logits[rows, g_sel], axis=-1)
    top_p, top_i = lax.top_k(e_p, TOP_K)
    gate = g_p * top_p / jnp.sum(top_p, axis=-1, keepdims=True)
    expert = (g_sel[:, None] * EXPERTS_PER_GROUP + top_i).astype(jnp.int32)
    n_assign = t_tok * TOP_K
    e_flat = expert.reshape(n_assign)
    tok_flat = jnp.arange(n_assign, dtype=jnp.int32) // TOP_K
    w_flat = gate.reshape(n_assign)
    order = jnp.argsort(e_flat)
    e_sorted = e_flat[order]
    counts = jnp.bincount(e_flat, length=N_EXPERTS)
    starts = jnp.cumsum(counts) - counts
    padded = (counts + MOE_BLOCK - 1) // MOE_BLOCK * MOE_BLOCK
    pends = jnp.cumsum(padded)
    dest = (pends - padded)[e_sorted] + jnp.arange(n_assign) - starts[e_sorted]
    n_rows = -(-(n_assign + N_EXPERTS * MOE_BLOCK) // MOE_BLOCK) * MOE_BLOCK
    n_blocks = n_rows // MOE_BLOCK
    row_tok = jnp.full((n_rows,), t_tok, jnp.int32).at[dest].set(tok_flat[order])
    row_w = jnp.zeros((n_rows,), jnp.float32).at[dest].set(w_flat[order])
    block_e = jnp.minimum(jnp.searchsorted(pends, jnp.arange(n_blocks) * MOE_BLOCK, side='right'), N_EXPERTS - 1)
    x_rows = jnp.concatenate([xt, jnp.zeros((1, d), xt.dtype)], axis=0)[row_tok]
    x_rows = x_rows.reshape(n_blocks, MOE_BLOCK, d)

    def expert_block(args):
        xb, e = args
        return (jax.nn.silu(xb @ w_gate[e]) * (xb @ w_up[e])) @ w_down[e]

    y_rows = lax.map(expert_block, (x_rows, block_e)).reshape(n_rows, d)
    y = jax.ops.segment_sum(y_rows * row_w[:, None], row_tok, num_segments=t_tok + 1)[:t_tok]
    return y.astype(h.dtype).reshape(bsz, s, d)


def setup_inputs(seed: int = 0) -> dict:
    key = jax.random.key(seed)
    k = jax.random.split(key, 32)
    f32 = jnp.float32

    def w(i, shape, fan_in):
        return jax.random.normal(k[i], shape, f32) * (fan_in ** -0.5)

    def gain(i, shape):
        return 1.0 + 0.05 * jax.random.normal(k[i], shape, f32)

    positions = (jax.random.randint(k[2], (BATCH, 1), 0, 4096, dtype=jnp.int32)
                 + jnp.arange(SEQ, dtype=jnp.int32)[None, :])
    return {
        'x': jax.random.normal(k[0], (BATCH, SEQ, D_MODEL), f32),
        'mem': jax.random.normal(k[1], (BATCH, MEM_TOKENS, D_MODEL), f32),
        'positions': positions,
        'norm_mix': gain(3, (DEPTH, D_MODEL)),
        'norm_mem': gain(4, (DEPTH, D_MODEL)),
        'norm_memtok': gain(5, (DEPTH, D_MODEL)),
        'norm_ffn': gain(6, (DEPTH, D_MODEL)),
        'pool_w': w(7, (N_A_LAYERS, N_POOL_GROUPS, POOL_GROUP, POOL_GROUP), POOL_GROUP),
        'pool_scale': gain(8, (N_A_LAYERS, D_MODEL)),
        'kv_in_norm': gain(9, (D_MODEL,)),
        'w_dkv': w(10, (D_MODEL, KV_RANK), D_MODEL),
        'kv_latent_norm': gain(11, (KV_RANK,)),
        'w_kr': w(12, (D_MODEL, ROPE_DIM), D_MODEL),
        'w_uk': w(13, (KV_RANK, MLA_HEADS * NOPE_DIM), KV_RANK),
        'w_uv': w(14, (KV_RANK, MLA_HEADS * V_DIM), KV_RANK),
        'w_dq': w(15, (N_B_LAYERS, D_MODEL, Q_RANK), D_MODEL),
        'q_latent_norm': gain(16, (N_B_LAYERS, Q_RANK)),
        'w_uq': w(17, (N_B_LAYERS, Q_RANK, MLA_HEADS * NOPE_DIM), Q_RANK),
        'w_qr': w(18, (N_B_LAYERS, Q_RANK, MLA_HEADS * ROPE_DIM), Q_RANK),
        'w_o': w(19, (N_B_LAYERS, MLA_HEADS * V_DIM, D_MODEL), MLA_HEADS * V_DIM),
        'mem_wq': w(20, (DEPTH, D_MODEL, MEM_INNER), D_MODEL),
        'mem_wk': w(21, (DEPTH, D_MODEL, MEM_INNER), D_MODEL),
        'mem_wv': w(22, (DEPTH, D_MODEL, MEM_INNER), D_MODEL),
        'mem_wo': w(23, (DEPTH, MEM_INNER, D_MODEL), MEM_INNER),
        'rg_w': w(24, (DEPTH, D_MODEL, N_GROUPS), D_MODEL),
        'rg_b': 0.01 * jax.random.normal(k[25], (DEPTH, N_GROUPS), f32),
        're_w': w(26, (DEPTH, D_MODEL, N_EXPERTS), D_MODEL),
        're_b': 0.01 * jax.random.normal(k[27], (DEPTH, N_EXPERTS), f32),
        'w_gate': w(28, (DEPTH, N_EXPERTS, D_MODEL, EXPERT_FF), D_MODEL),
        'w_up': w(29, (DEPTH, N_EXPERTS, D_MODEL, EXPERT_FF), D_MODEL),
        'w_down': w(30, (DEPTH, N_EXPERTS, EXPERT_FF, D_MODEL), EXPERT_FF),
        'final_norm': gain(31, (D_MODEL,)),
    }


def reference(x, mem, positions, norm_mix, norm_mem, norm_memtok, norm_ffn,
              pool_w, pool_scale, kv_in_norm, w_dkv, kv_latent_norm, w_kr, w_uk, w_uv,
              w_dq, q_latent_norm, w_uq, w_qr, w_o, mem_wq, mem_wk, mem_wv, mem_wo,
              rg_w, rg_b, re_w, re_b, w_gate, w_up, w_down, final_norm):
    cos, sin = rope_tables(positions)
    h = x
    for l in range(DEPTH):
        if l < N_A_LAYERS:
            h = h + pool_mixer(rmsnorm(h, norm_mix[l]), pool_w[l], pool_scale[l])
        else:
            if l == N_A_LAYERS:
                k_nope, k_rope, v_sh = mla_shared_kv(h, cos, sin, kv_in_norm, w_dkv,
                                                     kv_latent_norm, w_kr, w_uk, w_uv)
            j = l - N_A_LAYERS
            h = h + mla_attention(rmsnorm(h, norm_mix[l]), cos, sin, k_nope, k_rope, v_sh,
                                  w_dq[j], q_latent_norm[j], w_uq[j], w_qr[j], w_o[j])
        h = h + mem_attention(rmsnorm(h, norm_mem[l]), rmsnorm(mem, norm_memtok[l]),
                              mem_wq[l], mem_wk[l], mem_wv[l], mem_wo[l])
        h = h + hier_moe(rmsnorm(h, norm_ffn[l]), rg_w[l], rg_b[l], re_w[l], re_b[l],
                         w_gate[l], w_up[l], w_down[l])
    return rmsnorm(h, final_norm)
```

```python
import functools

import jax
import jax.numpy as jnp
from jax import lax
from jax.experimental import pallas as pl
from jax.experimental.pallas import tpu as pltpu

D_MODEL = 2048
DEPTH = 4
CHUNK = 64
N_A_LAYERS = DEPTH // 2
RMS_EPS = 1e-6
POOL_WINDOWS = (2, 4, 8, 16)
POOL_GROUP = D_MODEL // len(POOL_WINDOWS)
MLA_HEADS = 16
NOPE_DIM = 128
ROPE_DIM = 64
V_DIM = 128
ROPE_THETA = 10000.0
MEM_HEADS = 4
MEM_HEAD_DIM = 128
MEM_INNER = MEM_HEADS * MEM_HEAD_DIM
N_GROUPS = 4
EXPERTS_PER_GROUP = 8
N_EXPERTS = N_GROUPS * EXPERTS_PER_GROUP
TOP_K = 2
EXPERT_FF = D_MODEL // 4

LANES = 128
QK_PAD = 256
HALO = 16
POOL_SUB = 256
MOE_ROWS = 256
ROW_TILE = 512
ATT_TQ = 512
ATT_TK = 512
VMEM_LIMIT = 56 * 1024 * 1024

F32 = jnp.float32
BF16 = jnp.bfloat16


def _params(*sem):
    return pltpu.CompilerParams(dimension_semantics=sem, vmem_limit_bytes=VMEM_LIMIT)


def _rms(x, g):
    return x * lax.rsqrt(jnp.mean(x * x, axis=-1, keepdims=True) + RMS_EPS) * g


def _dot(a, b):
    return jnp.dot(a.astype(BF16), b.astype(BF16), preferred_element_type=F32)


def _dot_t(a, b):
    return lax.dot_general(a.astype(BF16), b.astype(BF16), (((1,), (1,)), ((), ())),
                           preferred_element_type=F32)


def _swap_halves(x):
    lane = lax.broadcasted_iota(jnp.int32, x.shape, 1)
    return jnp.where((lane & 32) == 0, pltpu.roll(x, 96, 1), pltpu.roll(x, 32, 1))


def _rope_kernel(pos_ref, cs_ref, sn_ref):
    pos = pos_ref[...].astype(F32)
    lane = lax.broadcasted_iota(jnp.int32, (1, LANES), 1)
    half = ROPE_DIM // 2
    frac = (lane % half).astype(F32) / half
    inv = 1.0 / jnp.power(jnp.full((1, LANES), ROPE_THETA, F32), frac)
    ang = pos * inv
    sign = jnp.where((lane & half) == 0, -1.0, 1.0)
    cs_ref[...] = jnp.cos(ang)
    sn_ref[...] = jnp.sin(ang) * sign


def _rope_tables(positions):
    b, s = positions.shape
    ts = ROW_TILE
    out = jax.ShapeDtypeStruct((b, s, LANES), F32)
    spec = pl.BlockSpec((None, ts, LANES), lambda i, j: (i, j, 0))
    return pl.pallas_call(
        _rope_kernel, grid=(b, s // ts),
        in_specs=[pl.BlockSpec((None, ts, 1), lambda i, j: (i, j, 0))],
        out_specs=[spec, spec], out_shape=[out, out],
        compiler_params=_params("parallel", "parallel"), name="rope_tables",
    )(positions.reshape(b, s, 1))


def _pool_kernel(h_ref, halo_ref, g_ref, w_ref, sc_ref, o_ref, buf_ref):
    ts = h_ref.shape[0]
    i = pl.program_id(1)
    g = g_ref[...]
    xn = _rms(h_ref[...], g)
    halo = _rms(halo_ref[...], g)
    buf_ref[0:HALO, :] = jnp.where(i > 0, halo, 0.0)
    buf_ref[HALO:HALO + ts, :] = xn
    r = POOL_SUB
    row = lax.broadcasted_iota(jnp.int32, (r, r + HALO), 0)
    col = lax.broadcasted_iota(jnp.int32, (r, r + HALO), 1)
    off = col - row
    for sub in range(ts // r):
        ext = buf_ref[sub * r:sub * r + r + HALO, :]
        ext_hi = ext.astype(BF16)
        ext_lo = (ext - ext_hi.astype(F32)).astype(BF16)
        t_seq = i * ts + sub * r + lax.broadcasted_iota(jnp.int32, (r, 1), 0)
        for gi, w in enumerate(POOL_WINDOWS):
            cols = slice(gi * POOL_GROUP, (gi + 1) * POOL_GROUP)
            band = jnp.where((off > HALO - w) & (off <= HALO), 1.0, 0.0).astype(BF16)
            wsum = (jnp.dot(band, ext_hi[:, cols], preferred_element_type=F32)
                    + jnp.dot(band, ext_lo[:, cols], preferred_element_type=F32))
            cnt = jnp.minimum(t_seq + 1, w).astype(F32)
            u = wsum / cnt - ext[HALO:, cols]
            y = _dot(u, w_ref[gi])
            rows = slice(sub * r, (sub + 1) * r)
            o_ref[rows, cols] = h_ref[rows, cols] + y * sc_ref[:, cols]


def _pool_layer(h, g, w, sc):
    b, s, d = h.shape
    ts = ROW_TILE
    hpt = ts // HALO
    return pl.pallas_call(
        _pool_kernel, grid=(b, s // ts),
        in_specs=[
            pl.BlockSpec((None, ts, d), lambda i, j: (i, j, 0)),
            pl.BlockSpec((None, HALO, d), lambda i, j: (i, jnp.maximum(j * hpt - 1, 0), 0)),
            pl.BlockSpec((1, d), lambda i, j: (0, 0)),
            pl.BlockSpec((len(POOL_WINDOWS), POOL_GROUP, POOL_GROUP), lambda i, j: (0, 0, 0)),
            pl.BlockSpec((1, d), lambda i, j: (0, 0)),
        ],
        out_specs=pl.BlockSpec((None, ts, d), lambda i, j: (i, j, 0)),
        out_shape=jax.ShapeDtypeStruct(h.shape, F32),
        scratch_shapes=[pltpu.VMEM((ts + HALO, d), F32)],
        compiler_params=_params("parallel", "arbitrary"), name="pool_layer",
    )(h, h, g.reshape(1, d), w.astype(BF16), sc.reshape(1, d))


def _mem_kv_kernel(m_ref, g_ref, wk_ref, wv_ref, k_ref, v_ref):
    m = _rms(m_ref[...], g_ref[...])
    k_ref[...] = _dot(m, wk_ref[...]).astype(BF16)
    v_ref[...] = _dot(m, wv_ref[...]).astype(BF16)


def _mem_kv(mem, g, wk, wv):
    b, m, d = mem.shape
    nl = g.shape[0]
    out = jax.ShapeDtypeStruct((nl, b, m, MEM_INNER), BF16)
    ospec = pl.BlockSpec((None, None, m, MEM_INNER), lambda l, i: (l, i, 0, 0))
    wspec = pl.BlockSpec((None, d, MEM_INNER), lambda l, i: (l, 0, 0))
    return pl.pallas_call(
        _mem_kv_kernel, grid=(nl, b),
        in_specs=[pl.BlockSpec((None, m, d), lambda l, i: (i, 0, 0)),
                  pl.BlockSpec((None, 1, d), lambda l, i: (l, 0, 0)), wspec, wspec],
        out_specs=[ospec, ospec], out_shape=[out, out],
        compiler_params=_params("arbitrary", "arbitrary"), name="mem_kv",
    )(mem, g.reshape(nl, 1, d), wk.astype(BF16), wv.astype(BF16))


def _mem_attn_kernel(h_ref, g_ref, wq_ref, k_ref, v_ref, wo_ref, o_ref):
    h = h_ref[...]
    q = _dot(_rms(h, g_ref[...]), wq_ref[...]).astype(BF16)
    scale = MEM_HEAD_DIM ** -0.5
    heads = []
    for hd in range(MEM_HEADS):
        cols = slice(hd * MEM_HEAD_DIM, (hd + 1) * MEM_HEAD_DIM)
        sc = _dot_t(q[:, cols], k_ref[:, cols]) * scale
        p = jnp.exp(sc - jnp.max(sc, axis=-1, keepdims=True))
        den = jnp.sum(p, axis=-1, keepdims=True)
        heads.append(_dot(p, v_ref[:, cols]) / den)
    o = jnp.concatenate(heads, axis=-1)
    o_ref[...] = h + _dot(o, wo_ref[...])


def _mem_attn(h, g, wq, k, v, wo, layer):
    b, s, d = h.shape
    ts = ROW_TILE
    m = k.shape[2]
    kvspec = pl.BlockSpec((None, None, m, MEM_INNER), lambda i, j: (layer, i, 0, 0))
    return pl.pallas_call(
        _mem_attn_kernel, grid=(b, s // ts),
        in_specs=[pl.BlockSpec((None, ts, d), lambda i, j: (i, j, 0)),
                  pl.BlockSpec((1, d), lambda i, j: (0, 0)),
                  pl.BlockSpec((d, MEM_INNER), lambda i, j: (0, 0)),
                  kvspec, kvspec,
                  pl.BlockSpec((MEM_INNER, d), lambda i, j: (0, 0))],
        out_specs=pl.BlockSpec((None, ts, d), lambda i, j: (i, j, 0)),
        out_shape=jax.ShapeDtypeStruct(h.shape, F32),
        compiler_params=_params("parallel", "parallel"), name="mem_attn",
    )(h, g.reshape(1, d), wq.astype(BF16), k, v, wo.astype(BF16))


def _router_kernel(h_ref, g_ref, w2_ref, wh_ref, b_ref, xn_ref, eid_ref, gate_ref):
    xn = _rms(h_ref[...], g_ref[...])
    xn_ref[...] = xn
    x_hi = xn.astype(BF16)
    x_lo = (xn - x_hi.astype(F32)).astype(BF16)
    a = jnp.dot(x_hi, w2_ref[...], preferred_element_type=F32)
    c = jnp.dot(x_lo, wh_ref[...], preferred_element_type=F32)
    logits = a[:, :LANES] + a[:, LANES:] + c + b_ref[...]
    lane = lax.broadcasted_iota(jnp.int32, logits.shape, 1)
    neg = -jnp.inf
    gl = jnp.where(lane < N_GROUPS, logits, neg)
    gmax = jnp.max(gl, axis=-1, keepdims=True)
    g_sel = jnp.min(jnp.where(gl == gmax, lane, LANES), axis=-1, keepdims=True)
    g_p = 1.0 / jnp.sum(jnp.exp(gl - gmax), axis=-1, keepdims=True)
    e_idx = lane - N_GROUPS
    in_grp = (e_idx >= 0) & (e_idx < N_EXPERTS) & ((e_idx >> 3) == g_sel)
    assert EXPERTS_PER_GROUP == 8
    el = jnp.where(in_grp, logits, neg)
    m1 = jnp.max(el, axis=-1, keepdims=True)
    i1 = jnp.min(jnp.where(el == m1, lane, LANES), axis=-1, keepdims=True)
    el2 = jnp.where(lane == i1, neg, el)
    m2 = jnp.max(el2, axis=-1, keepdims=True)
    i2 = jnp.min(jnp.where(el2 == m2, lane, LANES), axis=-1, keepdims=True)
    esum = jnp.sum(jnp.exp(el - m1), axis=-1, keepdims=True)
    p1 = 1.0 / esum
    p2 = jnp.exp(m2 - m1) / esum
    w1 = g_p * p1 / (p1 + p2)
    w2 = g_p * p2 / (p1 + p2)
    eid_ref[...] = jnp.where(lane == 0, i1 - N_GROUPS, jnp.where(lane == 1, i2 - N_GROUPS, 0))
    gate_ref[...] = jnp.where(lane == 0, w1, jnp.where(lane == 1, w2, 0.0))


def _router(h2, g, rg_w, rg_b, re_w, re_b):
    t, d = h2.shape
    ts = ROW_TILE
    pad = LANES - N_GROUPS - N_EXPERTS
    w = jnp.concatenate([rg_w, re_w, jnp.zeros((d, pad), F32)], axis=1)
    bias = jnp.concatenate([rg_b, re_b, jnp.zeros((pad,), F32)]).reshape(1, LANES)
    w_hi = w.astype(BF16)
    w_lo = (w - w_hi.astype(F32)).astype(BF16)
    w2 = jnp.concatenate([w_hi, w_lo], axis=1)
    row = lambda n, dt: jax.ShapeDtypeStruct((t, n), dt)
    return pl.pallas_call(
        _router_kernel, grid=(t // ts,),
        in_specs=[pl.BlockSpec((ts, d), lambda i: (i, 0)),
                  pl.BlockSpec((1, d), lambda i: (0, 0)),
                  pl.BlockSpec((d, 2 * LANES), lambda i: (0, 0)),
                  pl.BlockSpec((d, LANES), lambda i: (0, 0)),
                  pl.BlockSpec((1, LANES), lambda i: (0, 0))],
        out_specs=[pl.BlockSpec((ts, d), lambda i: (i, 0)),
                   pl.BlockSpec((ts, LANES), lambda i: (i, 0)),
                   pl.BlockSpec((ts, LANES), lambda i: (i, 0))],
        out_shape=[row(d, F32), row(LANES, jnp.int32), row(LANES, F32)],
        compiler_params=_params("parallel"), name="router",
    )(h2, g.reshape(1, d), w2, w_hi, bias)


def _dispatch_plan(eid, t):
    n_assign = t * TOP_K
    e_flat = eid[:, :TOP_K].reshape(n_assign)
    order = jnp.argsort(e_flat).astype(jnp.int32)
    e_sorted = e_flat[order]
    counts = jnp.bincount(e_flat, length=N_EXPERTS)
    starts = jnp.cumsum(counts) - counts
    padded = (counts + MOE_ROWS - 1) // MOE_ROWS * MOE_ROWS
    pends = jnp.cumsum(padded)
    dest = (pends - padded)[e_sorted] + jnp.arange(n_assign) - starts[e_sorted]
    n_rows = -(-(n_assign + N_EXPERTS * MOE_ROWS) // MOE_ROWS) * MOE_ROWS
    n_blocks = n_rows // MOE_ROWS
    local = jnp.arange(n_rows, dtype=jnp.int32) % MOE_ROWS
    slot_row = (order % TOP_K) * t + order // TOP_K
    src_tok = jnp.zeros((n_rows,), jnp.int32).at[dest].set(order // TOP_K)
    dst_row = (n_assign + local).at[dest].set(slot_row)
    block_e = jnp.minimum(jnp.searchsorted(pends, jnp.arange(n_blocks) * MOE_ROWS, side='right'),
                          N_EXPERTS - 1).astype(jnp.int32)
    n_used = (pends[-1] // MOE_ROWS).astype(jnp.int32).reshape(1)
    return src_tok, dst_row, block_e, n_used, n_blocks


def _expert_kernel(be_ref, nu_ref, src_ref, dst_ref, x_hbm, wg_ref, wu_ref, wd_ref, y_hbm,
                   xbuf, ybuf, gsem, ssem):
    blk = pl.program_id(0)

    @pl.when(blk == 0)
    def _():
        ybuf[...] = jnp.zeros(ybuf.shape, F32)
        dump = pltpu.make_async_copy(ybuf, y_hbm.at[pl.ds(y_hbm.shape[0] - MOE_ROWS, MOE_ROWS)], ssem)
        dump.start()
        dump.wait()

    @pl.when(blk < nu_ref[0])
    def _():
        for r in range(MOE_ROWS):
            pltpu.make_async_copy(x_hbm.at[pl.ds(src_ref[0, r], 1)],
                                  xbuf.at[pl.ds(r, 1)], gsem).start()
        for r in range(MOE_ROWS):
            pltpu.make_async_copy(x_hbm.at[pl.ds(0, 1)], xbuf.at[pl.ds(0, 1)], gsem).wait()
        x = xbuf[...].astype(BF16)
        gate = jnp.dot(x, wg_ref[...], preferred_element_type=F32)
        up = jnp.dot(x, wu_ref[...], preferred_element_type=F32)
        mid = (gate * jax.nn.sigmoid(gate)) * up
        ybuf[...] = jnp.dot(mid.astype(BF16), wd_ref[...], preferred_element_type=F32)
        for r in range(MOE_ROWS):
            pltpu.make_async_copy(ybuf.at[pl.ds(r, 1)],
                                  y_hbm.at[pl.ds(dst_ref[0, r], 1)], ssem).start()
        for r in range(MOE_ROWS):
            pltpu.make_async_copy(ybuf.at[pl.ds(0, 1)], y_hbm.at[pl.ds(0, 1)], ssem).wait()


def _experts(xn, plan, wg, wu, wd):
    src_tok, dst_row, block_e, n_used, n_blocks = plan
    t, d = xn.shape
    ff = wg.shape[-1]
    wspec = lambda shape: pl.BlockSpec((None,) + shape, lambda i, be, nu: (be[i], 0, 0))
    ispec = pl.BlockSpec((None, 1, MOE_ROWS), lambda i, be, nu: (i, 0, 0), memory_space=pltpu.SMEM)
    grid_spec = pltpu.PrefetchScalarGridSpec(
        num_scalar_prefetch=2, grid=(n_blocks,),
        in_specs=[ispec, ispec, pl.BlockSpec(memory_space=pl.ANY),
                  wspec((d, ff)), wspec((d, ff)), wspec((ff, d))],
        out_specs=pl.BlockSpec(memory_space=pl.ANY),
        scratch_shapes=[pltpu.VMEM((MOE_ROWS, d), F32), pltpu.VMEM((MOE_ROWS, d), F32),
                        pltpu.SemaphoreType.DMA(()), pltpu.SemaphoreType.DMA(())])
    return pl.pallas_call(
        _expert_kernel, grid_spec=grid_spec,
        out_shape=jax.ShapeDtypeStruct((TOP_K * t + MOE_ROWS, d), F32),
        compiler_params=_params("arbitrary"), name="experts",
    )(block_e, n_used, src_tok.reshape(n_blocks, 1, MOE_ROWS), dst_row.reshape(n_blocks, 1, MOE_ROWS),
      xn, wg.astype(BF16), wu.astype(BF16), wd.astype(BF16))


def _combine_kernel(h_ref, y0_ref, y1_ref, gate_ref, g_ref, o_ref, *, final):
    gate = gate_ref[...]
    out = h_ref[...] + gate[:, 0:1] * y0_ref[...] + gate[:, 1:2] * y1_ref[...]
    o_ref[...] = _rms(out, g_ref[...]) if final else out


def _combine(h2, y, gate, g, final):
    t, d = h2.shape
    ts = ROW_TILE
    nt = t // ts
    return pl.pallas_call(
        functools.partial(_combine_kernel, final=final), grid=(nt,),
        in_specs=[pl.BlockSpec((ts, d), lambda i: (i, 0)),
                  pl.BlockSpec((ts, d), lambda i: (i, 0)),
                  pl.BlockSpec((ts, d), lambda i: (i + nt, 0)),
                  pl.BlockSpec((ts, LANES), lambda i: (i, 0)),
                  pl.BlockSpec((1, d), lambda i: (0, 0))],
        out_specs=pl.BlockSpec((ts, d), lambda i: (i, 0)),
        out_shape=jax.ShapeDtypeStruct(h2.shape, F32),
        compiler_params=_params("parallel"), name="moe_combine",
    )(h2, y, y, gate, g.reshape(1, d))


def _moe(h, g, rg_w, rg_b, re_w, re_b, wg, wu, wd, final_g, final):
    b, s, d = h.shape
    h2 = h.reshape(b * s, d)
    xn, eid, gate = _router(h2, g, rg_w, rg_b, re_w, re_b)
    plan = _dispatch_plan(eid, b * s)
    y = _experts(xn, plan, wg, wu, wd)
    return _combine(h2, y, gate, final_g, final).reshape(b, s, d)


def _kv_kernel(h_ref, gin_ref, wdkv_ref, glat_ref, wkr_ref, wuk_ref, wuv_ref, cs_ref, sn_ref,
               k_ref, v_ref):
    hn = _rms(h_ref[...], gin_ref[...])
    ckv = _rms(_dot(hn, wdkv_ref[...]), glat_ref[...])
    kr = _dot(hn, wkr_ref[...])
    kr = kr * cs_ref[...] + _swap_halves(kr) * sn_ref[...]
    lane = lax.broadcasted_iota(jnp.int32, kr.shape, 1)
    kr = jnp.where(lane < ROPE_DIM, kr, 0.0).astype(BF16)
    kn = _dot(ckv, wuk_ref[...]).astype(BF16)
    vv = _dot(ckv, wuv_ref[...]).astype(BF16)
    for hd in range(MLA_HEADS):
        k_ref[hd, :, 0:NOPE_DIM] = kn[:, hd * NOPE_DIM:(hd + 1) * NOPE_DIM]
        k_ref[hd, :, NOPE_DIM:QK_PAD] = kr
        v_ref[hd] = vv[:, hd * V_DIM:(hd + 1) * V_DIM]


def _shared_kv(h, cs, sn, gin, wdkv, glat, wkr, wuk, wuv):
    b, s, d = h.shape
    ts = ROW_TILE
    r = wdkv.shape[1]
    wkr_p = jnp.concatenate([wkr, jnp.zeros((d, LANES - ROPE_DIM), F32)], axis=1).astype(BF16)
    full = lambda shape: pl.BlockSpec(shape, lambda i, j: (0,) * len(shape))
    tab = pl.BlockSpec((None, ts, LANES), lambda i, j: (i, j, 0))
    return pl.pallas_call(
        _kv_kernel, grid=(b, s // ts),
        in_specs=[pl.BlockSpec((None, ts, d), lambda i, j: (i, j, 0)),
                  full((1, d)), full((d, r)), full((1, r)), full((d, LANES)),
                  full((r, MLA_HEADS * NOPE_DIM)), full((r, MLA_HEADS * V_DIM)), tab, tab],
        out_specs=[pl.BlockSpec((None, MLA_HEADS, ts, QK_PAD), lambda i, j: (i, 0, j, 0)),
                   pl.BlockSpec((None, MLA_HEADS, ts, V_DIM), lambda i, j: (i, 0, j, 0))],
        out_shape=[jax.ShapeDtypeStruct((b, MLA_HEADS, s, QK_PAD), BF16),
                   jax.ShapeDtypeStruct((b, MLA_HEADS, s, V_DIM), BF16)],
        compiler_params=_params("parallel", "parallel"), name="mla_shared_kv",
    )(h, gin.reshape(1, d), wdkv.astype(BF16), glat.reshape(1, r), wkr_p,
      wuk.astype(BF16), wuv.astype(BF16), cs, sn)


def _q_kernel(h_ref, g_ref, wdq_ref, glat_ref, wuq_ref, wqr_ref, cs_ref, sn_ref, q_ref):
    xn = _rms(h_ref[...], g_ref[...])
    cq = _rms(_dot(xn, wdq_ref[...]), glat_ref[...])
    scale = (NOPE_DIM + ROPE_DIM) ** -0.5
    qn = (_dot(cq, wuq_ref[...]) * scale).astype(BF16)
    qr = _dot(cq, wqr_ref[...]) * scale
    cs = cs_ref[...]
    sn = sn_ref[...]
    lane = lax.broadcasted_iota(jnp.int32, cs.shape, 1)
    for pair in range(MLA_HEADS // 2):
        x = qr[:, pair * LANES:(pair + 1) * LANES]
        rot = x * cs + _swap_halves(x) * sn
        for odd in range(2):
            hd = 2 * pair + odd
            part = pltpu.roll(rot, ROPE_DIM, 1) if odd else rot
            q_ref[hd, :, 0:NOPE_DIM] = qn[:, hd * NOPE_DIM:(hd + 1) * NOPE_DIM]
            q_ref[hd, :, NOPE_DIM:QK_PAD] = jnp.where(lane < ROPE_DIM, part, 0.0).astype(BF16)


def _q_proj(h, cs, sn, g, wdq, glat, wuq, wqr):
    b, s, d = h.shape
    ts = ROW_TILE
    r = wdq.shape[1]
    full = lambda shape: pl.BlockSpec(shape, lambda i, j: (0,) * len(shape))
    tab = pl.BlockSpec((None, ts, LANES), lambda i, j: (i, j, 0))
    return pl.pallas_call(
        _q_kernel, grid=(b, s // ts),
        in_specs=[pl.BlockSpec((None, ts, d), lambda i, j: (i, j, 0)),
                  full((1, d)), full((d, r)), full((1, r)),
                  full((r, MLA_HEADS * NOPE_DIM)), full((r, MLA_HEADS * ROPE_DIM)), tab, tab],
        out_specs=pl.BlockSpec((None, MLA_HEADS, ts, QK_PAD), lambda i, j: (i, 0, j, 0)),
        out_shape=jax.ShapeDtypeStruct((b, MLA_HEADS, s, QK_PAD), BF16),
        compiler_params=_params("parallel", "parallel"), name="mla_q_proj",
    )(h, g.reshape(1, d), wdq.astype(BF16), glat.reshape(1, r), wuq.astype(BF16),
      wqr.astype(BF16), cs, sn)


def _flash_kernel(q_ref, k_ref, v_ref, o_ref, m_scr, l_scr, acc_scr):
    qi = pl.program_id(1)
    ki = pl.program_id(2)
    tq = q_ref.shape[1]
    tk = k_ref.shape[1]

    @pl.when(ki == 0)
    def _():
        m_scr[...] = jnp.full(m_scr.shape, -jnp.inf, F32)
        l_scr[...] = jnp.zeros(l_scr.shape, F32)
        acc_scr[...] = jnp.zeros(acc_scr.shape, F32)

    @pl.when(ki <= qi)
    def _():
        rowc = (qi * tq + lax.broadcasted_iota(jnp.int32, (tq, tk), 0)) // CHUNK
        colc = (ki * tk + lax.broadcasted_iota(jnp.int32, (tq, tk), 1)) // CHUNK
        bias = jnp.where(colc <= rowc, 0.0, -jnp.inf)

        def head(hd, carry):
            s = lax.dot_general(q_ref[hd], k_ref[hd], (((1,), (1,)), ((), ())),
                                preferred_element_type=F32) + bias
            m_old = m_scr[hd]
            m_new = jnp.maximum(m_old, jnp.max(s, axis=-1, keepdims=True))
            alpha = jnp.exp(m_old - m_new)
            p = jnp.exp(s - m_new[:, 0:1])
            l_scr[hd] = alpha * l_scr[hd] + jnp.sum(p, axis=-1, keepdims=True)
            acc_scr[hd] = alpha * acc_scr[hd] + jnp.dot(p.astype(BF16), v_ref[hd],
                                                        preferred_element_type=F32)
            m_scr[hd] = m_new
            return carry

        lax.fori_loop(0, MLA_HEADS, head, 0)

    @pl.when(ki == qi)
    def _():
        for hd in range(MLA_HEADS):
            o_ref[:, hd * V_DIM:(hd + 1) * V_DIM] = (acc_scr[hd] / l_scr[hd]).astype(BF16)


def _flash(q, k, v):
    b, nh, s, _ = q.shape
    tq, tk = ATT_TQ, ATT_TK
    assert tq == tk and tq % CHUNK == 0
    return pl.pallas_call(
        _flash_kernel, grid=(b, s // tq, s // tk),
        in_specs=[pl.BlockSpec((None, nh, tq, QK_PAD), lambda i, qi, ki: (i, 0, qi, 0)),
                  pl.BlockSpec((None, nh, tk, QK_PAD), lambda i, qi, ki: (i, 0, jnp.minimum(ki, qi), 0)),
                  pl.BlockSpec((None, nh, tk, V_DIM), lambda i, qi, ki: (i, 0, jnp.minimum(ki, qi), 0))],
        out_specs=pl.BlockSpec((None, tq, nh * V_DIM), lambda i, qi, ki: (i, qi, 0)),
        out_shape=jax.ShapeDtypeStruct((b, s, nh * V_DIM), BF16),
        scratch_shapes=[pltpu.VMEM((nh, tq, LANES), F32), pltpu.VMEM((nh, tq, LANES), F32),
                        pltpu.VMEM((nh, tq, V_DIM), F32)],
        compiler_params=_params("parallel", "parallel", "arbitrary"), name="mla_flash",
    )(q, k, v)


def _oproj_kernel(h_ref, o_ref, w_ref, out_ref):
    out_ref[...] = h_ref[...] + jnp.dot(o_ref[...], w_ref[...], preferred_element_type=F32)


def _o_proj(h, o, w):
    b, s, d = h.shape
    ts = ROW_TILE
    tile = lambda n: pl.BlockSpec((None, ts, n), lambda i, j: (i, j, 0))
    return pl.pallas_call(
        _oproj_kernel, grid=(b, s // ts),
        in_specs=[tile(d), tile(o.shape[-1]), pl.BlockSpec(w.shape, lambda i, j: (0, 0))],
        out_specs=tile(d), out_shape=jax.ShapeDtypeStruct(h.shape, F32),
        compiler_params=_params("parallel", "parallel"), name="mla_o_proj",
    )(h, o, w.astype(BF16))


def kernel(x, mem, positions, norm_mix, norm_mem, norm_memtok, norm_ffn, pool_w, pool_scale, kv_in_norm, w_dkv, kv_latent_norm, w_kr, w_uk, w_uv, w_dq, q_latent_norm, w_uq, w_qr, w_o, mem_wq, mem_wk, mem_wv, mem_wo, rg_w, rg_b, re_w, re_b, w_gate, w_up, w_down, final_norm):
    cs, sn = _rope_tables(positions)
    mk, mv = _mem_kv(mem, norm_memtok, mem_wk, mem_wv)
    h = x
    k_sh = v_sh = None
    for l in range(DEPTH):
        if l < N_A_LAYERS:
            h = _pool_layer(h, norm_mix[l], pool_w[l], pool_scale[l])
        else:
            if l == N_A_LAYERS:
                k_sh, v_sh = _shared_kv(h, cs, sn, kv_in_norm, w_dkv, kv_latent_norm, w_kr, w_uk, w_uv)
            j = l - N_A_LAYERS
            q = _q_proj(h, cs, sn, norm_mix[l], w_dq[j], q_latent_norm[j], w_uq[j], w_qr[j])
            h = _o_proj(h, _flash(q, k_sh, v_sh), w_o[j])
        h = _mem_attn(h, norm_mem[l], mem_wq[l], mk, mv, mem_wo[l], l)
        h = _moe(h, norm_ffn[l], rg_w[l], rg_b[l], re_w[l], re_b[l], w_gate[l], w_up[l], w_down[l],
                 final_norm, l == DEPTH - 1)
    return h
```

```python
import functools

import jax
import jax.numpy as jnp
from jax import lax
from jax.experimental import pallas as pl
from jax.experimental.pallas import tpu as pltpu

D_MODEL = 2048
DEPTH = 4
CHUNK = 64
N_A_LAYERS = DEPTH // 2
RMS_EPS = 1e-6
POOL_WINDOWS = (2, 4, 8, 16)
POOL_GROUP = D_MODEL // len(POOL_WINDOWS)
MLA_HEADS = 16
NOPE_DIM = 128
ROPE_DIM = 64
V_DIM = 128
ROPE_THETA = 10000.0
MEM_HEADS = 4
MEM_HEAD_DIM = 128
MEM_INNER = MEM_HEADS * MEM_HEAD_DIM
N_GROUPS = 4
EXPERTS_PER_GROUP = 8
N_EXPERTS = N_GROUPS * EXPERTS_PER_GROUP
TOP_K = 2
EXPERT_FF = D_MODEL // 4

LANES = 128
QK_PAD = 256
HALO = 16
POOL_SUB = 256
MOE_ROWS = 256
ROW_TILE = 512
ATT_TQ = 512
ATT_TK = 512
FLASH_HEADS_PER_ITER = 4
LOG2_E = 1.4426950408889634
VMEM_LIMIT = 56 * 1024 * 1024

F32 = jnp.float32
BF16 = jnp.bfloat16


def _params(*sem):
    return pltpu.CompilerParams(dimension_semantics=sem, vmem_limit_bytes=VMEM_LIMIT)


def _rms(x, g):
    return x * lax.rsqrt(jnp.mean(x * x, axis=-1, keepdims=True) + RMS_EPS) * g


def _dot(a, b):
    return jnp.dot(a.astype(BF16), b.astype(BF16), preferred_element_type=F32)


def _dot_t(a, b):
    return lax.dot_general(a.astype(BF16), b.astype(BF16), (((1,), (1,)), ((), ())),
                           preferred_element_type=F32)


def _swap_halves(x):
    lane = lax.broadcasted_iota(jnp.int32, x.shape, 1)
    return jnp.where((lane & 32) == 0, pltpu.roll(x, 96, 1), pltpu.roll(x, 32, 1))


def _rope_kernel(pos_ref, cs_ref, sn_ref):
    pos = pos_ref[...].astype(F32)
    lane = lax.broadcasted_iota(jnp.int32, (1, LANES), 1)
    half = ROPE_DIM // 2
    frac = (lane % half).astype(F32) / half
    inv = 1.0 / jnp.power(jnp.full((1, LANES), ROPE_THETA, F32), frac)
    ang = pos * inv
    sign = jnp.where((lane & half) == 0, -1.0, 1.0)
    cs_ref[...] = jnp.cos(ang)
    sn_ref[...] = jnp.sin(ang) * sign


def _rope_tables(positions):
    b, s = positions.shape
    ts = ROW_TILE
    out = jax.ShapeDtypeStruct((b, s, LANES), F32)
    spec = pl.BlockSpec((None, ts, LANES), lambda i, j: (i, j, 0))
    return pl.pallas_call(
        _rope_kernel, grid=(b, s // ts),
        in_specs=[pl.BlockSpec((None, ts, 1), lambda i, j: (i, j, 0))],
        out_specs=[spec, spec], out_shape=[out, out],
        compiler_params=_params("parallel", "parallel"), name="rope_tables",
    )(positions.reshape(b, s, 1))


def _pool_kernel(h_ref, halo_ref, g_ref, w_ref, sc_ref, o_ref, buf_ref):
    ts = h_ref.shape[0]
    i = pl.program_id(1)
    g = g_ref[...]
    xn = _rms(h_ref[...], g)
    halo = _rms(halo_ref[...], g)
    buf_ref[0:HALO, :] = jnp.where(i > 0, halo, 0.0)
    buf_ref[HALO:HALO + ts, :] = xn
    r = POOL_SUB
    row = lax.broadcasted_iota(jnp.int32, (r, r + HALO), 0)
    col = lax.broadcasted_iota(jnp.int32, (r, r + HALO), 1)
    off = col - row
    for sub in range(ts // r):
        ext = buf_ref[sub * r:sub * r + r + HALO, :]
        ext_hi = ext.astype(BF16)
        ext_lo = (ext - ext_hi.astype(F32)).astype(BF16)
        t_seq = i * ts + sub * r + lax.broadcasted_iota(jnp.int32, (r, 1), 0)
        for gi, w in enumerate(POOL_WINDOWS):
            cols = slice(gi * POOL_GROUP, (gi + 1) * POOL_GROUP)
            band = jnp.where((off > HALO - w) & (off <= HALO), 1.0, 0.0).astype(BF16)
            wsum = (jnp.dot(band, ext_hi[:, cols], preferred_element_type=F32)
                    + jnp.dot(band, ext_lo[:, cols], preferred_element_type=F32))
            cnt = jnp.minimum(t_seq + 1, w).astype(F32)
            u = wsum / cnt - ext[HALO:, cols]
            y = _dot(u, w_ref[gi])
            rows = slice(sub * r, (sub + 1) * r)
            o_ref[rows, cols] = h_ref[rows, cols] + y * sc_ref[:, cols]


def _pool_layer(h, g, w, sc):
    b, s, d = h.shape
    ts = ROW_TILE
    hpt = ts // HALO
    return pl.pallas_call(
        _pool_kernel, grid=(b, s // ts),
        in_specs=[
            pl.BlockSpec((None, ts, d), lambda i, j: (i, j, 0)),
            pl.BlockSpec((None, HALO, d), lambda i, j: (i, jnp.maximum(j * hpt - 1, 0), 0)),
            pl.BlockSpec((1, d), lambda i, j: (0, 0)),
            pl.BlockSpec((len(POOL_WINDOWS), POOL_GROUP, POOL_GROUP), lambda i, j: (0, 0, 0)),
            pl.BlockSpec((1, d), lambda i, j: (0, 0)),
        ],
        out_specs=pl.BlockSpec((None, ts, d), lambda i, j: (i, j, 0)),
        out_shape=jax.ShapeDtypeStruct(h.shape, F32),
        scratch_shapes=[pltpu.VMEM((ts + HALO, d), F32)],
        compiler_params=_params("parallel", "arbitrary"), name="pool_layer",
    )(h, h, g.reshape(1, d), w.astype(BF16), sc.reshape(1, d))


def _mem_kv_kernel(m_ref, g_ref, wk_ref, wv_ref, k_ref, v_ref):
    m = _rms(m_ref[...], g_ref[...])
    k_ref[...] = _dot(m, wk_ref[...]).astype(BF16)
    v_ref[...] = _dot(m, wv_ref[...]).astype(BF16)


def _mem_kv(mem, g, wk, wv):
    b, m, d = mem.shape
    nl = g.shape[0]
    out = jax.ShapeDtypeStruct((nl, b, m, MEM_INNER), BF16)
    ospec = pl.BlockSpec((None, None, m, MEM_INNER), lambda l, i: (l, i, 0, 0))
    wspec = pl.BlockSpec((None, d, MEM_INNER), lambda l, i: (l, 0, 0))
    return pl.pallas_call(
        _mem_kv_kernel, grid=(nl, b),
        in_specs=[pl.BlockSpec((None, m, d), lambda l, i: (i, 0, 0)),
                  pl.BlockSpec((None, 1, d), lambda l, i: (l, 0, 0)), wspec, wspec],
        out_specs=[ospec, ospec], out_shape=[out, out],
        compiler_params=_params("arbitrary", "arbitrary"), name="mem_kv",
    )(mem, g.reshape(nl, 1, d), wk.astype(BF16), wv.astype(BF16))


def _mem_attn_kernel(h_ref, g_ref, wq_ref, k_ref, v_ref, wo_ref, o_ref):
    h = h_ref[...]
    q = _dot(_rms(h, g_ref[...]), wq_ref[...]).astype(BF16)
    scale = MEM_HEAD_DIM ** -0.5
    heads = []
    for hd in range(MEM_HEADS):
        cols = slice(hd * MEM_HEAD_DIM, (hd + 1) * MEM_HEAD_DIM)
        sc = _dot_t(q[:, cols], k_ref[:, cols]) * scale
        p = jnp.exp(sc - jnp.max(sc, axis=-1, keepdims=True))
        den = jnp.sum(p, axis=-1, keepdims=True)
        heads.append(_dot(p, v_ref[:, cols]) / den)
    o = jnp.concatenate(heads, axis=-1)
    o_ref[...] = h + _dot(o, wo_ref[...])


def _mem_attn(h, g, wq, k, v, wo, layer):
    b, s, d = h.shape
    ts = ROW_TILE
    m = k.shape[2]
    kvspec = pl.BlockSpec((None, None, m, MEM_INNER), lambda i, j: (layer, i, 0, 0))
    return pl.pallas_call(
        _mem_attn_kernel, grid=(b, s // ts),
        in_specs=[pl.BlockSpec((None, ts, d), lambda i, j: (i, j, 0)),
                  pl.BlockSpec((1, d), lambda i, j: (0, 0)),
                  pl.BlockSpec((d, MEM_INNER), lambda i, j: (0, 0)),
                  kvspec, kvspec,
                  pl.BlockSpec((MEM_INNER, d), lambda i, j: (0, 0))],
        out_specs=pl.BlockSpec((None, ts, d), lambda i, j: (i, j, 0)),
        out_shape=jax.ShapeDtypeStruct(h.shape, F32),
        compiler_params=_params("parallel", "parallel"), name="mem_attn",
    )(h, g.reshape(1, d), wq.astype(BF16), k, v, wo.astype(BF16))


def _router_kernel(h_ref, g_ref, w2_ref, wh_ref, b_ref, xn_ref, eid_ref, gate_ref):
    xn = _rms(h_ref[...], g_ref[...])
    xn_ref[...] = xn
    x_hi = xn.astype(BF16)
    x_lo = (xn - x_hi.astype(F32)).astype(BF16)
    a = jnp.dot(x_hi, w2_ref[...], preferred_element_type=F32)
    c = jnp.dot(x_lo, wh_ref[...], preferred_element_type=F32)
    logits = a[:, :LANES] + a[:, LANES:] + c + b_ref[...]
    lane = lax.broadcasted_iota(jnp.int32, logits.shape, 1)
    neg = -jnp.inf
    gl = jnp.where(lane < N_GROUPS, logits, neg)
    gmax = jnp.max(gl, axis=-1, keepdims=True)
    g_sel = jnp.min(jnp.where(gl == gmax, lane, LANES), axis=-1, keepdims=True)
    g_p = 1.0 / jnp.sum(jnp.exp(gl - gmax), axis=-1, keepdims=True)
    e_idx = lane - N_GROUPS
    in_grp = (e_idx >= 0) & (e_idx < N_EXPERTS) & ((e_idx >> 3) == g_sel)
    assert EXPERTS_PER_GROUP == 8
    el = jnp.where(in_grp, logits, neg)
    m1 = jnp.max(el, axis=-1, keepdims=True)
    i1 = jnp.min(jnp.where(el == m1, lane, LANES), axis=-1, keepdims=True)
    el2 = jnp.where(lane == i1, neg, el)
    m2 = jnp.max(el2, axis=-1, keepdims=True)
    i2 = jnp.min(jnp.where(el2 == m2, lane, LANES), axis=-1, keepdims=True)
    esum = jnp.sum(jnp.exp(el - m1), axis=-1, keepdims=True)
    p1 = 1.0 / esum
    p2 = jnp.exp(m2 - m1) / esum
    w1 = g_p * p1 / (p1 + p2)
    w2 = g_p * p2 / (p1 + p2)
    eid_ref[...] = jnp.where(lane == 0, i1 - N_GROUPS, jnp.where(lane == 1, i2 - N_GROUPS, 0))
    gate_ref[...] = jnp.where(lane == 0, w1, jnp.where(lane == 1, w2, 0.0))


def _router(h2, g, rg_w, rg_b, re_w, re_b):
    t, d = h2.shape
    ts = ROW_TILE
    pad = LANES - N_GROUPS - N_EXPERTS
    w = jnp.concatenate([rg_w, re_w, jnp.zeros((d, pad), F32)], axis=1)
    bias = jnp.concatenate([rg_b, re_b, jnp.zeros((pad,), F32)]).reshape(1, LANES)
    w_hi = w.astype(BF16)
    w_lo = (w - w_hi.astype(F32)).astype(BF16)
    w2 = jnp.concatenate([w_hi, w_lo], axis=1)
    row = lambda n, dt: jax.ShapeDtypeStruct((t, n), dt)
    return pl.pallas_call(
        _router_kernel, grid=(t // ts,),
        in_specs=[pl.BlockSpec((ts, d), lambda i: (i, 0)),
                  pl.BlockSpec((1, d), lambda i: (0, 0)),
                  pl.BlockSpec((d, 2 * LANES), lambda i: (0, 0)),
                  pl.BlockSpec((d, LANES), lambda i: (0, 0)),
                  pl.BlockSpec((1, LANES), lambda i: (0, 0))],
        out_specs=[pl.BlockSpec((ts, d), lambda i: (i, 0)),
                   pl.BlockSpec((ts, LANES), lambda i: (i, 0)),
                   pl.BlockSpec((ts, LANES), lambda i: (i, 0))],
        out_shape=[row(d, F32), row(LANES, jnp.int32), row(LANES, F32)],
        compiler_params=_params("parallel"), name="router",
    )(h2, g.reshape(1, d), w2, w_hi, bias)


META_BLOCK_E, META_N_USED, META_FILL_START, META_FILL_COUNT = 0, 1, 2, 3
META_LANES = 2 * LANES


def _onehot(eid, k):
    lane = lax.broadcasted_iota(jnp.int32, eid.shape, 1)
    return lane == eid[:, k:k + 1]


def _plan_kernel(eid_ref, dest_ref, meta_ref, cnt_scr, carry_scr):
    ph = pl.program_id(0)
    i = pl.program_id(1)
    ts = eid_ref.shape[0]
    eid = eid_ref[...]
    o0 = _onehot(eid, 0)
    o1 = _onehot(eid, 1)
    osum = jnp.where(o0, 1.0, 0.0) + jnp.where(o1, 1.0, 0.0)
    tile_cnt = jnp.sum(osum, axis=0, keepdims=True)

    @pl.when((ph == 0) & (i == 0))
    def _():
        cnt_scr[...] = jnp.zeros(cnt_scr.shape, F32)

    @pl.when(ph == 0)
    def _():
        cnt_scr[...] = cnt_scr[...] + tile_cnt

    @pl.when(ph == 1)
    def _():
        cnt = cnt_scr[...]
        nblk = jnp.floor((cnt + (MOE_ROWS - 1)) * (1.0 / MOE_ROWS))
        er = lax.broadcasted_iota(jnp.int32, (LANES, LANES), 0)
        ec = lax.broadcasted_iota(jnp.int32, (LANES, LANES), 1)
        before = jnp.where(er < ec, 1.0, 0.0).astype(BF16)
        bstart = jnp.dot(nblk.astype(BF16), before, preferred_element_type=F32)
        row_start = bstart * MOE_ROWS

        @pl.when(i == 0)
        def _():
            carry_scr[...] = jnp.zeros(carry_scr.shape, F32)
            bend = bstart + nblk
            bend_col = jnp.sum(jnp.where(er == ec, bend[0:1], 0.0), axis=1, keepdims=True)
            e2 = lax.broadcasted_iota(jnp.int32, (LANES, META_LANES), 0)
            b2 = lax.broadcasted_iota(jnp.int32, (LANES, META_LANES), 1).astype(F32)
            done = jnp.where((bend_col <= b2) & (e2 < N_EXPERTS), 1.0, 0.0)
            block_e = jnp.minimum(jnp.sum(done, axis=0, keepdims=True), N_EXPERTS - 1.0)
            n_used = jnp.sum(nblk[0:1], axis=1, keepdims=True)
            zpad = jnp.zeros((1, META_LANES - LANES), F32)
            fill_start = jnp.concatenate([row_start[0:1] + cnt[0:1], zpad], axis=1)
            fill_count = jnp.concatenate([nblk[0:1] * MOE_ROWS - cnt[0:1], zpad], axis=1)
            row = lax.broadcasted_iota(jnp.int32, meta_ref.shape, 0)
            meta = jnp.where(row == META_BLOCK_E, block_e,
                             jnp.where(row == META_N_USED, n_used,
                                       jnp.where(row == META_FILL_START, fill_start,
                                                 jnp.where(row == META_FILL_COUNT, fill_count, 0.0))))
            meta_ref[...] = meta.astype(jnp.int32)

        rr = lax.broadcasted_iota(jnp.int32, (ts, ts), 0)
        rc = lax.broadcasted_iota(jnp.int32, (ts, ts), 1)
        earlier = jnp.where(rc < rr, 1.0, 0.0).astype(BF16)
        base = (jnp.dot(earlier, osum.astype(BF16), preferred_element_type=F32)
                + carry_scr[0:1] + row_start[0:1])
        d0 = jnp.sum(jnp.where(o0, base, 0.0), axis=1, keepdims=True)
        d1 = jnp.sum(jnp.where(o1, base + jnp.where(o0, 1.0, 0.0), 0.0), axis=1, keepdims=True)
        lane = lax.broadcasted_iota(jnp.int32, (ts, LANES), 1)
        dest_ref[...] = jnp.where(lane == 0, d0, jnp.where(lane == 1, d1, 0.0)).astype(jnp.int32)
        carry_scr[...] = carry_scr[...] + tile_cnt


def _plan(eid):
    t = eid.shape[0]
    ts = ROW_TILE
    dest, meta = pl.pallas_call(
        _plan_kernel, grid=(2, t // ts),
        in_specs=[pl.BlockSpec((ts, LANES), lambda ph, i: (i, 0))],
        out_specs=[pl.BlockSpec((ts, LANES), lambda ph, i: (ph * i, 0)),
                   pl.BlockSpec((8, META_LANES), lambda ph, i: (0, 0))],
        out_shape=[jax.ShapeDtypeStruct((t, LANES), jnp.int32),
                   jax.ShapeDtypeStruct((8, META_LANES), jnp.int32)],
        scratch_shapes=[pltpu.VMEM((8, LANES), F32), pltpu.VMEM((8, LANES), F32)],
        compiler_params=_params("arbitrary", "arbitrary"), name="moe_plan",
    )(eid)
    dest_rows = dest[:, :TOP_K].reshape(t // ts, 1, TOP_K * ts)
    return dest_rows, meta


def _dispatch_kernel(meta_ref, dest_ref, x_hbm, xr_hbm, zrow, zblk, zsem, sem):
    i = pl.program_id(0)
    ts = dest_ref.shape[1] // TOP_K

    @pl.when(i == 0)
    def _():
        zrow[...] = jnp.zeros(zrow.shape, F32)
        for e in range(N_EXPERTS):
            start = meta_ref[META_FILL_START, e]
            count = meta_ref[META_FILL_COUNT, e]

            def fill(j, c, start=start):
                pltpu.make_async_copy(zrow, xr_hbm.at[pl.ds(start + j, 1)], zsem).start()
                return c

            def drain(j, c):
                pltpu.make_async_copy(zrow, xr_hbm.at[pl.ds(0, 1)], zsem).wait()
                return c

            lax.fori_loop(0, count, fill, 0)
            lax.fori_loop(0, count, drain, 0)

        zblk[...] = jnp.zeros(zblk.shape, F32)
        n_used = meta_ref[META_N_USED, 0]
        n_blocks = xr_hbm.shape[0] // MOE_ROWS

        def tail(b, c):
            cp = pltpu.make_async_copy(zblk, xr_hbm.at[pl.ds(pl.multiple_of(b * MOE_ROWS, MOE_ROWS), MOE_ROWS)], zsem)
            cp.start()
            cp.wait()
            return c

        lax.fori_loop(n_used, n_blocks, tail, 0)

    def send(t, c):
        src = x_hbm.at[pl.ds(i * ts + t, 1)]
        for k in range(TOP_K):
            pltpu.make_async_copy(src, xr_hbm.at[pl.ds(dest_ref[0, TOP_K * t + k], 1)], sem).start()
        return c

    def done(t, c):
        for k in range(TOP_K):
            pltpu.make_async_copy(x_hbm.at[pl.ds(0, 1)], xr_hbm.at[pl.ds(0, 1)], sem).wait()
        return c

    lax.fori_loop(0, ts, send, 0, unroll=8)
    lax.fori_loop(0, ts, done, 0, unroll=8)


def _dispatch(xn, dest_rows, meta, n_rows):
    t, d = xn.shape
    nt = dest_rows.shape[0]
    grid_spec = pltpu.PrefetchScalarGridSpec(
        num_scalar_prefetch=1, grid=(nt,),
        in_specs=[pl.BlockSpec((None, 1, dest_rows.shape[2]), lambda i, m: (i, 0, 0),
                               memory_space=pltpu.SMEM),
                  pl.BlockSpec(memory_space=pl.ANY)],
        out_specs=pl.BlockSpec(memory_space=pl.ANY),
        scratch_shapes=[pltpu.VMEM((1, d), F32), pltpu.VMEM((MOE_ROWS, d), F32),
                        pltpu.SemaphoreType.DMA(()), pltpu.SemaphoreType.DMA(())])
    return pl.pallas_call(
        _dispatch_kernel, grid_spec=grid_spec,
        out_shape=jax.ShapeDtypeStruct((n_rows, d), F32),
        compiler_params=_params("arbitrary"), name="moe_dispatch",
    )(meta, dest_rows, xn)


def _expert_kernel(meta_ref, x_ref, wg_ref, wu_ref, wd_ref, y_ref):
    blk = pl.program_id(0)

    @pl.when(blk < meta_ref[META_N_USED, 0])
    def _():
        x = x_ref[...].astype(BF16)
        gate = jnp.dot(x, wg_ref[...], preferred_element_type=F32)
        up = jnp.dot(x, wu_ref[...], preferred_element_type=F32)
        mid = (gate * jax.nn.sigmoid(gate)) * up
        y_ref[...] = jnp.dot(mid.astype(BF16), wd_ref[...], preferred_element_type=F32)

    @pl.when(blk >= meta_ref[META_N_USED, 0])
    def _():
        y_ref[...] = jnp.zeros(y_ref.shape, F32)


def _experts(x_rows, meta, wg, wu, wd):
    n_rows, d = x_rows.shape
    ff = wg.shape[-1]
    wspec = lambda shape: pl.BlockSpec((None,) + shape, lambda i, m: (m[META_BLOCK_E, i], 0, 0))
    grid_spec = pltpu.PrefetchScalarGridSpec(
        num_scalar_prefetch=1, grid=(n_rows // MOE_ROWS,),
        in_specs=[pl.BlockSpec((MOE_ROWS, d), lambda i, m: (jnp.minimum(i, m[META_N_USED, 0] - 1), 0)),
                  wspec((d, ff)), wspec((d, ff)), wspec((ff, d))],
        out_specs=pl.BlockSpec((MOE_ROWS, d), lambda i, m: (i, 0)))
    return pl.pallas_call(
        _expert_kernel, grid_spec=grid_spec,
        out_shape=jax.ShapeDtypeStruct((n_rows, d), F32),
        compiler_params=_params("arbitrary"), name="experts",
    )(meta, x_rows, wg.astype(BF16), wu.astype(BF16), wd.astype(BF16))


def _combine_kernel(dest_ref, h_ref, gate_ref, g_ref, y_hbm, o_ref, y0buf, y1buf, sem, *, final):
    ts = h_ref.shape[0]
    bufs = (y0buf, y1buf)

    def fetch(t, c):
        for k in range(TOP_K):
            pltpu.make_async_copy(y_hbm.at[pl.ds(dest_ref[0, TOP_K * t + k], 1)],
                                  bufs[k].at[pl.ds(t, 1)], sem).start()
        return c

    def done(t, c):
        for k in range(TOP_K):
            pltpu.make_async_copy(y_hbm.at[pl.ds(0, 1)], bufs[k].at[pl.ds(0, 1)], sem).wait()
        return c

    lax.fori_loop(0, ts, fetch, 0, unroll=8)
    lax.fori_loop(0, ts, done, 0, unroll=8)
    gate = gate_ref[...]
    out = h_ref[...] + gate[:, 0:1] * y0buf[...] + gate[:, 1:2] * y1buf[...]
    o_ref[...] = _rms(out, g_ref[...]) if final else out


def _combine(h2, y_rows, dest_rows, gate, g, final):
    t, d = h2.shape
    nt = dest_rows.shape[0]
    ts = t // nt
    assert TOP_K == 2
    return pl.pallas_call(
        functools.partial(_combine_kernel, final=final), grid=(nt,),
        in_specs=[pl.BlockSpec((None, 1, TOP_K * ts), lambda i: (i, 0, 0), memory_space=pltpu.SMEM),
                  pl.BlockSpec((ts, d), lambda i: (i, 0)),
                  pl.BlockSpec((ts, LANES), lambda i: (i, 0)),
                  pl.BlockSpec((1, d), lambda i: (0, 0)),
                  pl.BlockSpec(memory_space=pl.ANY)],
        out_specs=pl.BlockSpec((ts, d), lambda i: (i, 0)),
        out_shape=jax.ShapeDtypeStruct(h2.shape, F32),
        scratch_shapes=[pltpu.VMEM((ts, d), F32), pltpu.VMEM((ts, d), F32), pltpu.SemaphoreType.DMA(())],
        compiler_params=_params("arbitrary"), name="moe_combine",
    )(dest_rows, h2, gate, g.reshape(1, d), y_rows)


def _moe(h, g, rg_w, rg_b, re_w, re_b, wg, wu, wd, final_g, final):
    b, s, d = h.shape
    t = b * s
    h2 = h.reshape(t, d)
    xn, eid, gate = _router(h2, g, rg_w, rg_b, re_w, re_b)
    dest_rows, meta = _plan(eid)
    n_rows = -(-(t * TOP_K + N_EXPERTS * (MOE_ROWS - 1)) // MOE_ROWS) * MOE_ROWS
    assert n_rows // MOE_ROWS <= META_LANES
    x_rows = _dispatch(xn, dest_rows, meta, n_rows)
    y_rows = _experts(x_rows, meta, wg, wu, wd)
    return _combine(h2, y_rows, dest_rows, gate, final_g, final).reshape(b, s, d)


def _kv_kernel(h_ref, gin_ref, wdkv_ref, glat_ref, wkr_ref, wuk_ref, wuv_ref, cs_ref, sn_ref,
               k_ref, v_ref):
    hn = _rms(h_ref[...], gin_ref[...])
    ckv = _rms(_dot(hn, wdkv_ref[...]), glat_ref[...])
    kr = _dot(hn, wkr_ref[...])
    kr = kr * cs_ref[...] + _swap_halves(kr) * sn_ref[...]
    lane = lax.broadcasted_iota(jnp.int32, kr.shape, 1)
    kr = jnp.where(lane < ROPE_DIM, kr, 0.0).astype(BF16)
    kn = _dot(ckv, wuk_ref[...]).astype(BF16)
    vv = _dot(ckv, wuv_ref[...]).astype(BF16)
    for hd in range(MLA_HEADS):
        k_ref[hd, :, 0:NOPE_DIM] = kn[:, hd * NOPE_DIM:(hd + 1) * NOPE_DIM]
        k_ref[hd, :, NOPE_DIM:QK_PAD] = kr
        v_ref[hd] = vv[:, hd * V_DIM:(hd + 1) * V_DIM]


def _shared_kv(h, cs, sn, gin, wdkv, glat, wkr, wuk, wuv):
    b, s, d = h.shape
    ts = ROW_TILE
    r = wdkv.shape[1]
    wkr_p = jnp.concatenate([wkr, jnp.zeros((d, LANES - ROPE_DIM), F32)], axis=1).astype(BF16)
    full = lambda shape: pl.BlockSpec(shape, lambda i, j: (0,) * len(shape))
    tab = pl.BlockSpec((None, ts, LANES), lambda i, j: (i, j, 0))
    return pl.pallas_call(
        _kv_kernel, grid=(b, s // ts),
        in_specs=[pl.BlockSpec((None, ts, d), lambda i, j: (i, j, 0)),
                  full((1, d)), full((d, r)), full((1, r)), full((d, LANES)),
                  full((r, MLA_HEADS * NOPE_DIM)), full((r, MLA_HEADS * V_DIM)), tab, tab],
        out_specs=[pl.BlockSpec((None, MLA_HEADS, ts, QK_PAD), lambda i, j: (i, 0, j, 0)),
                   pl.BlockSpec((None, MLA_HEADS, ts, V_DIM), lambda i, j: (i, 0, j, 0))],
        out_shape=[jax.ShapeDtypeStruct((b, MLA_HEADS, s, QK_PAD), BF16),
                   jax.ShapeDtypeStruct((b, MLA_HEADS, s, V_DIM), BF16)],
        compiler_params=_params("parallel", "parallel"), name="mla_shared_kv",
    )(h, gin.reshape(1, d), wdkv.astype(BF16), glat.reshape(1, r), wkr_p,
      wuk.astype(BF16), wuv.astype(BF16), cs, sn)


def _q_kernel(h_ref, g_ref, wdq_ref, glat_ref, wuq_ref, wqr_ref, cs_ref, sn_ref, q_ref):
    xn = _rms(h_ref[...], g_ref[...])
    cq = _rms(_dot(xn, wdq_ref[...]), glat_ref[...])
    scale = (NOPE_DIM + ROPE_DIM) ** -0.5 * LOG2_E
    qn = (_dot(cq, wuq_ref[...]) * scale).astype(BF16)
    qr = _dot(cq, wqr_ref[...]) * scale
    cs = cs_ref[...]
    sn = sn_ref[...]
    lane = lax.broadcasted_iota(jnp.int32, cs.shape, 1)
    for pair in range(MLA_HEADS // 2):
        x = qr[:, pair * LANES:(pair + 1) * LANES]
        rot = x * cs + _swap_halves(x) * sn
        for odd in range(2):
            hd = 2 * pair + odd
            part = pltpu.roll(rot, ROPE_DIM, 1) if odd else rot
            q_ref[hd, :, 0:NOPE_DIM] = qn[:, hd * NOPE_DIM:(hd + 1) * NOPE_DIM]
            q_ref[hd, :, NOPE_DIM:QK_PAD] = jnp.where(lane < ROPE_DIM, part, 0.0).astype(BF16)


def _q_proj(h, cs, sn, g, wdq, glat, wuq, wqr):
    b, s, d = h.shape
    ts = ROW_TILE
    r = wdq.shape[1]
    full = lambda shape: pl.BlockSpec(shape, lambda i, j: (0,) * len(shape))
    tab = pl.BlockSpec((None, ts, LANES), lambda i, j: (i, j, 0))
    return pl.pallas_call(
        _q_kernel, grid=(b, s // ts),
        in_specs=[pl.BlockSpec((None, ts, d), lambda i, j: (i, j, 0)),
                  full((1, d)), full((d, r)), full((1, r)),
                  full((r, MLA_HEADS * NOPE_DIM)), full((r, MLA_HEADS * ROPE_DIM)), tab, tab],
        out_specs=pl.BlockSpec((None, MLA_HEADS, ts, QK_PAD), lambda i, j: (i, 0, j, 0)),
        out_shape=jax.ShapeDtypeStruct((b, MLA_HEADS, s, QK_PAD), BF16),
        compiler_params=_params("parallel", "parallel"), name="mla_q_proj",
    )(h, g.reshape(1, d), wdq.astype(BF16), glat.reshape(1, r), wuq.astype(BF16),
      wqr.astype(BF16), cs, sn)


def _flash_kernel(q_ref, k_ref, v_ref, o_ref, m_scr, l_scr, acc_scr):
    qi = pl.program_id(1)
    ki = pl.program_id(2)
    tq = q_ref.shape[1]
    tk = k_ref.shape[1]

    @pl.when(ki == 0)
    def _():
        m_scr[...] = jnp.full(m_scr.shape, -jnp.inf, F32)
        l_scr[...] = jnp.zeros(l_scr.shape, F32)
        acc_scr[...] = jnp.zeros(acc_scr.shape, F32)

    def one_head(hd, bias):
        s = lax.dot_general(q_ref[hd], k_ref[hd], (((1,), (1,)), ((), ())),
                            preferred_element_type=F32)
        if bias is not None:
            s = s + bias
        m_old = m_scr[hd]
        m_new = jnp.maximum(m_old, jnp.max(s, axis=-1, keepdims=True))
        alpha = jnp.exp2(m_old - m_new)
        p = jnp.exp2(s - m_new[:, 0:1])
        l_scr[hd] = alpha * l_scr[hd] + jnp.sum(p, axis=-1, keepdims=True)
        acc_scr[hd] = alpha * acc_scr[hd] + jnp.dot(p.astype(BF16), v_ref[hd],
                                                    preferred_element_type=F32)
        m_scr[hd] = m_new

    def all_heads(bias):
        def group(g, carry):
            for j in range(FLASH_HEADS_PER_ITER):
                one_head(g * FLASH_HEADS_PER_ITER + j, bias)
            return carry
        lax.fori_loop(0, MLA_HEADS // FLASH_HEADS_PER_ITER, group, 0)

    @pl.when(ki < qi)
    def _():
        all_heads(None)

    @pl.when(ki == qi)
    def _():
        rowc = lax.broadcasted_iota(jnp.int32, (tq, tk), 0) // CHUNK
        colc = lax.broadcasted_iota(jnp.int32, (tq, tk), 1) // CHUNK
        all_heads(jnp.where(colc <= rowc, 0.0, -jnp.inf))
        for hd in range(MLA_HEADS):
            o_ref[:, hd * V_DIM:(hd + 1) * V_DIM] = (acc_scr[hd] / l_scr[hd]).astype(BF16)


def _flash(q, k, v):
    b, nh, s, _ = q.shape
    tq, tk = ATT_TQ, ATT_TK
    assert tq == tk and tq % CHUNK == 0
    return pl.pallas_call(
        _flash_kernel, grid=(b, s // tq, s // tk),
        in_specs=[pl.BlockSpec((None, nh, tq, QK_PAD), lambda i, qi, ki: (i, 0, qi, 0)),
                  pl.BlockSpec((None, nh, tk, QK_PAD), lambda i, qi, ki: (i, 0, jnp.minimum(ki, qi), 0)),
                  pl.BlockSpec((None, nh, tk, V_DIM), lambda i, qi, ki: (i, 0, jnp.minimum(ki, qi), 0))],
        out_specs=pl.BlockSpec((None, tq, nh * V_DIM), lambda i, qi, ki: (i, qi, 0)),
        out_shape=jax.ShapeDtypeStruct((b, s, nh * V_DIM), BF16),
        scratch_shapes=[pltpu.VMEM((nh, tq, LANES), F32), pltpu.VMEM((nh, tq, LANES), F32),
                        pltpu.VMEM((nh, tq, V_DIM), F32)],
        compiler_params=_params("parallel", "parallel", "arbitrary"), name="mla_flash",
    )(q, k, v)


def _oproj_kernel(h_ref, o_ref, w_ref, out_ref):
    out_ref[...] = h_ref[...] + jnp.dot(o_ref[...], w_ref[...], preferred_element_type=F32)


def _o_proj(h, o, w):
    b, s, d = h.shape
    ts = ROW_TILE
    tile = lambda n: pl.BlockSpec((None, ts, n), lambda i, j: (i, j, 0))
    return pl.pallas_call(
        _oproj_kernel, grid=(b, s // ts),
        in_specs=[tile(d), tile(o.shape[-1]), pl.BlockSpec(w.shape, lambda i, j: (0, 0))],
        out_specs=tile(d), out_shape=jax.ShapeDtypeStruct(h.shape, F32),
        compiler_params=_params("parallel", "parallel"), name="mla_o_proj",
    )(h, o, w.astype(BF16))


def kernel(x, mem, positions, norm_mix, norm_mem, norm_memtok, norm_ffn, pool_w, pool_scale, kv_in_norm, w_dkv, kv_latent_norm, w_kr, w_uk, w_uv, w_dq, q_latent_norm, w_uq, w_qr, w_o, mem_wq, mem_wk, mem_wv, mem_wo, rg_w, rg_b, re_w, re_b, w_gate, w_up, w_down, final_norm):
    cs, sn = _rope_tables(positions)
    mk, mv = _mem_kv(mem, norm_memtok, mem_wk, mem_wv)
    h = x
    k_sh = v_sh = None
    for l in range(DEPTH):
        if l < N_A_LAYERS:
            h = _pool_layer(h, norm_mix[l], pool_w[l], pool_scale[l])
        else:
            if l == N_A_LAYERS:
                k_sh, v_sh = _shared_kv(h, cs, sn, kv_in_norm, w_dkv, kv_latent_norm, w_kr, w_uk, w_uv)
            j = l - N_A_LAYERS
            q = _q_proj(h, cs, sn, norm_mix[l], w_dq[j], q_latent_norm[j], w_uq[j], w_qr[j])
            h = _o_proj(h, _flash(q, k_sh, v_sh), w_o[j])
        h = _mem_attn(h, norm_mem[l], mem_wq[l], mk, mv, mem_wo[l], l)
        h = _moe(h, norm_ffn[l], rg_w[l], rg_b[l], re_w[l], re_b[l], w_gate[l], w_up[l], w_down[l],
                 final_norm, l == DEPTH - 1)
    return h
```

```python
import functools

import jax
import jax.numpy as jnp
from jax import lax
from jax.experimental import pallas as pl
from jax.experimental.pallas import tpu as pltpu

D_MODEL = 2048
DEPTH = 4
CHUNK = 64
N_A_LAYERS = DEPTH // 2
RMS_EPS = 1e-6
POOL_WINDOWS = (2, 4, 8, 16)
POOL_GROUP = D_MODEL // len(POOL_WINDOWS)
MLA_HEADS = 16
NOPE_DIM = 128
ROPE_DIM = 64
V_DIM = 128
ROPE_THETA = 10000.0
MEM_HEADS = 4
MEM_HEAD_DIM = 128
MEM_INNER = MEM_HEADS * MEM_HEAD_DIM
N_GROUPS = 4
EXPERTS_PER_GROUP = 8
N_EXPERTS = N_GROUPS * EXPERTS_PER_GROUP
TOP_K = 2
EXPERT_FF = D_MODEL // 4

LANES = 128
QK_PAD = 256
HALO = 16
POOL_SUB = 256
MOE_ROWS = 256
ROW_TILE = 512
ATT_TQ = 512
ATT_TK = 512
LOG2_E = 1.4426950408889634
VMEM_LIMIT = 56 * 1024 * 1024

F32 = jnp.float32
BF16 = jnp.bfloat16


def _params(*sem):
    return pltpu.CompilerParams(dimension_semantics=sem, vmem_limit_bytes=VMEM_LIMIT)


def _rms(x, g):
    return x * lax.rsqrt(jnp.mean(x * x, axis=-1, keepdims=True) + RMS_EPS) * g


def _dot(a, b):
    return jnp.dot(a.astype(BF16), b.astype(BF16), preferred_element_type=F32)


def _dot_t(a, b):
    return lax.dot_general(a.astype(BF16), b.astype(BF16), (((1,), (1,)), ((), ())),
                           preferred_element_type=F32)


def _weight_spec(shape, layer=None):
    if layer is None:
        return pl.BlockSpec(shape, lambda *_: (0,) * len(shape), pipeline_mode=pl.Buffered(1))
    return pl.BlockSpec((None,) + shape, lambda *_: (layer,) + (0,) * len(shape),
                        pipeline_mode=pl.Buffered(1))


def _cast_weights_once(first, pairs):
    @pl.when(first)
    def _():
        for src, dst in pairs:
            dst[...] = src[...].astype(BF16)


def _swap_halves(x):
    lane = lax.broadcasted_iota(jnp.int32, x.shape, 1)
    return jnp.where((lane & 32) == 0, pltpu.roll(x, 96, 1), pltpu.roll(x, 32, 1))


def _rope_kernel(pos_ref, cs_ref, sn_ref):
    pos = pos_ref[...].astype(F32)
    lane = lax.broadcasted_iota(jnp.int32, (1, LANES), 1)
    half = ROPE_DIM // 2
    frac = (lane % half).astype(F32) / half
    inv = 1.0 / jnp.power(jnp.full((1, LANES), ROPE_THETA, F32), frac)
    ang = pos * inv
    sign = jnp.where((lane & half) == 0, -1.0, 1.0)
    cs_ref[...] = jnp.cos(ang)
    sn_ref[...] = jnp.sin(ang) * sign


def _rope_tables(positions):
    b, s = positions.shape
    ts = ROW_TILE
    out = jax.ShapeDtypeStruct((b, s, LANES), F32)
    spec = pl.BlockSpec((None, ts, LANES), lambda i, j: (i, j, 0))
    return pl.pallas_call(
        _rope_kernel, grid=(b, s // ts),
        in_specs=[pl.BlockSpec((None, ts, 1), lambda i, j: (i, j, 0))],
        out_specs=[spec, spec], out_shape=[out, out],
        compiler_params=_params("parallel", "parallel"), name="rope_tables",
    )(positions.reshape(b, s, 1))


def _pool_kernel(h_ref, halo_ref, g_ref, wf_ref, sc_ref, o_ref, buf_ref, w_ref):
    ts = h_ref.shape[0]
    i = pl.program_id(1)
    _cast_weights_once((pl.program_id(0) == 0) & (i == 0), [(wf_ref, w_ref)])
    g = g_ref[...]
    xn = _rms(h_ref[...], g)
    halo = _rms(halo_ref[...], g)
    buf_ref[0:HALO, :] = jnp.where(i > 0, halo, 0.0)
    buf_ref[HALO:HALO + ts, :] = xn
    r = POOL_SUB
    row = lax.broadcasted_iota(jnp.int32, (r, r + HALO), 0)
    col = lax.broadcasted_iota(jnp.int32, (r, r + HALO), 1)
    off = col - row
    for sub in range(ts // r):
        ext = buf_ref[sub * r:sub * r + r + HALO, :]
        ext_hi = ext.astype(BF16)
        ext_lo = (ext - ext_hi.astype(F32)).astype(BF16)
        t_seq = i * ts + sub * r + lax.broadcasted_iota(jnp.int32, (r, 1), 0)
        for gi, w in enumerate(POOL_WINDOWS):
            cols = slice(gi * POOL_GROUP, (gi + 1) * POOL_GROUP)
            band = jnp.where((off > HALO - w) & (off <= HALO), 1.0, 0.0).astype(BF16)
            wsum = (jnp.dot(band, ext_hi[:, cols], preferred_element_type=F32)
                    + jnp.dot(band, ext_lo[:, cols], preferred_element_type=F32))
            cnt = jnp.minimum(t_seq + 1, w).astype(F32)
            u = wsum / cnt - ext[HALO:, cols]
            y = _dot(u, w_ref[gi])
            rows = slice(sub * r, (sub + 1) * r)
            o_ref[rows, cols] = h_ref[rows, cols] + y * sc_ref[:, cols]


def _pool_layer(h, g, w_all, sc, layer):
    b, s, d = h.shape
    ts = ROW_TILE
    hpt = ts // HALO
    wshape = w_all.shape[1:]
    return pl.pallas_call(
        _pool_kernel, grid=(b, s // ts),
        in_specs=[
            pl.BlockSpec((None, ts, d), lambda i, j: (i, j, 0)),
            pl.BlockSpec((None, HALO, d), lambda i, j: (i, jnp.maximum(j * hpt - 1, 0), 0)),
            pl.BlockSpec((1, d), lambda i, j: (0, 0)),
            _weight_spec(wshape, layer),
            pl.BlockSpec((1, d), lambda i, j: (0, 0)),
        ],
        out_specs=pl.BlockSpec((None, ts, d), lambda i, j: (i, j, 0)),
        out_shape=jax.ShapeDtypeStruct(h.shape, F32),
        scratch_shapes=[pltpu.VMEM((ts + HALO, d), F32), pltpu.VMEM(wshape, BF16)],
        compiler_params=_params("arbitrary", "arbitrary"), name="pool_layer",
    )(h, h, g.reshape(1, d), w_all, sc.reshape(1, d))


def _mem_kv_kernel(m_ref, g_ref, wk_ref, wv_ref, k_ref, v_ref):
    m = _rms(m_ref[...], g_ref[...])
    k_ref[...] = _dot(m, wk_ref[...]).astype(BF16)
    v_ref[...] = _dot(m, wv_ref[...]).astype(BF16)


def _mem_kv(mem, g, wk, wv):
    b, m, d = mem.shape
    nl = g.shape[0]
    out = jax.ShapeDtypeStruct((nl, b, m, MEM_INNER), BF16)
    ospec = pl.BlockSpec((None, None, m, MEM_INNER), lambda l, i: (l, i, 0, 0))
    wspec = pl.BlockSpec((None, d, MEM_INNER), lambda l, i: (l, 0, 0))
    return pl.pallas_call(
        _mem_kv_kernel, grid=(nl, b),
        in_specs=[pl.BlockSpec((None, m, d), lambda l, i: (i, 0, 0)),
                  pl.BlockSpec((None, 1, d), lambda l, i: (l, 0, 0)), wspec, wspec],
        out_specs=[ospec, ospec], out_shape=[out, out],
        compiler_params=_params("arbitrary", "arbitrary"), name="mem_kv",
    )(mem, g.reshape(nl, 1, d), wk, wv)


def _mem_attn_kernel(h_ref, g_ref, wqf_ref, k_ref, v_ref, wof_ref, o_ref, wq_ref, wo_ref):
    _cast_weights_once((pl.program_id(0) == 0) & (pl.program_id(1) == 0),
                       [(wqf_ref, wq_ref), (wof_ref, wo_ref)])
    h = h_ref[...]
    q = _dot(_rms(h, g_ref[...]), wq_ref[...]).astype(BF16)
    scale = MEM_HEAD_DIM ** -0.5
    heads = []
    for hd in range(MEM_HEADS):
        cols = slice(hd * MEM_HEAD_DIM, (hd + 1) * MEM_HEAD_DIM)
        sc = _dot_t(q[:, cols], k_ref[:, cols]) * scale
        p = jnp.exp(sc - jnp.max(sc, axis=-1, keepdims=True))
        den = jnp.sum(p, axis=-1, keepdims=True)
        heads.append(_dot(p, v_ref[:, cols]) / den)
    o = jnp.concatenate(heads, axis=-1)
    o_ref[...] = h + _dot(o, wo_ref[...])


def _mem_attn(h, g, wq, k, v, wo, layer):
    b, s, d = h.shape
    ts = ROW_TILE
    m = k.shape[2]
    kvspec = pl.BlockSpec((None, None, m, MEM_INNER), lambda i, j: (layer, i, 0, 0))
    return pl.pallas_call(
        _mem_attn_kernel, grid=(b, s // ts),
        in_specs=[pl.BlockSpec((None, ts, d), lambda i, j: (i, j, 0)),
                  pl.BlockSpec((1, d), lambda i, j: (0, 0)),
                  _weight_spec((d, MEM_INNER), layer),
                  kvspec, kvspec,
                  _weight_spec((MEM_INNER, d), layer)],
        out_specs=pl.BlockSpec((None, ts, d), lambda i, j: (i, j, 0)),
        out_shape=jax.ShapeDtypeStruct(h.shape, F32),
        scratch_shapes=[pltpu.VMEM((d, MEM_INNER), BF16), pltpu.VMEM((MEM_INNER, d), BF16)],
        compiler_params=_params("arbitrary", "arbitrary"), name="mem_attn",
    )(h, g.reshape(1, d), wq, k, v, wo)


def _router_kernel(h_ref, g_ref, w2_ref, wh_ref, b_ref, xn_ref, eid_ref, gate_ref):
    xn = _rms(h_ref[...], g_ref[...])
    xn_ref[...] = xn
    x_hi = xn.astype(BF16)
    x_lo = (xn - x_hi.astype(F32)).astype(BF16)
    a = jnp.dot(x_hi, w2_ref[...], preferred_element_type=F32)
    c = jnp.dot(x_lo, wh_ref[...], preferred_element_type=F32)
    logits = a[:, :LANES] + a[:, LANES:] + c + b_ref[...]
    lane = lax.broadcasted_iota(jnp.int32, logits.shape, 1)
    neg = -jnp.inf
    gl = jnp.where(lane < N_GROUPS, logits, neg)
    gmax = jnp.max(gl, axis=-1, keepdims=True)
    g_sel = jnp.min(jnp.where(gl == gmax, lane, LANES), axis=-1, keepdims=True)
    g_p = 1.0 / jnp.sum(jnp.exp(gl - gmax), axis=-1, keepdims=True)
    e_idx = lane - N_GROUPS
    in_grp = (e_idx >= 0) & (e_idx < N_EXPERTS) & ((e_idx >> 3) == g_sel)
    assert EXPERTS_PER_GROUP == 8
    el = jnp.where(in_grp, logits, neg)
    m1 = jnp.max(el, axis=-1, keepdims=True)
    i1 = jnp.min(jnp.where(el == m1, lane, LANES), axis=-1, keepdims=True)
    el2 = jnp.where(lane == i1, neg, el)
    m2 = jnp.max(el2, axis=-1, keepdims=True)
    i2 = jnp.min(jnp.where(el2 == m2, lane, LANES), axis=-1, keepdims=True)
    esum = jnp.sum(jnp.exp(el - m1), axis=-1, keepdims=True)
    p1 = 1.0 / esum
    p2 = jnp.exp(m2 - m1) / esum
    w1 = g_p * p1 / (p1 + p2)
    w2 = g_p * p2 / (p1 + p2)
    eid_ref[...] = jnp.where(lane == 0, i1 - N_GROUPS, jnp.where(lane == 1, i2 - N_GROUPS, 0))
    gate_ref[...] = jnp.where(lane == 0, w1, jnp.where(lane == 1, w2, 0.0))


def _router(h2, g, rg_w, rg_b, re_w, re_b):
    t, d = h2.shape
    ts = ROW_TILE
    pad = LANES - N_GROUPS - N_EXPERTS
    w = jnp.concatenate([rg_w, re_w, jnp.zeros((d, pad), F32)], axis=1)
    bias = jnp.concatenate([rg_b, re_b, jnp.zeros((pad,), F32)]).reshape(1, LANES)
    w_hi = w.astype(BF16)
    w_lo = (w - w_hi.astype(F32)).astype(BF16)
    w2 = jnp.concatenate([w_hi, w_lo], axis=1)
    row = lambda n, dt: jax.ShapeDtypeStruct((t, n), dt)
    return pl.pallas_call(
        _router_kernel, grid=(t // ts,),
        in_specs=[pl.BlockSpec((ts, d), lambda i: (i, 0)),
                  pl.BlockSpec((1, d), lambda i: (0, 0)),
                  pl.BlockSpec((d, 2 * LANES), lambda i: (0, 0)),
                  pl.BlockSpec((d, LANES), lambda i: (0, 0)),
                  pl.BlockSpec((1, LANES), lambda i: (0, 0))],
        out_specs=[pl.BlockSpec((ts, d), lambda i: (i, 0)),
                   pl.BlockSpec((ts, LANES), lambda i: (i, 0)),
                   pl.BlockSpec((ts, LANES), lambda i: (i, 0))],
        out_shape=[row(d, F32), row(LANES, jnp.int32), row(LANES, F32)],
        compiler_params=_params("parallel"), name="router",
    )(h2, g.reshape(1, d), w2, w_hi, bias)


META_BLOCK_E, META_N_USED, META_FILL_START, META_FILL_COUNT = 0, 1, 2, 3
META_LANES = 2 * LANES


def _onehot(eid, k):
    lane = lax.broadcasted_iota(jnp.int32, eid.shape, 1)
    return lane == eid[:, k:k + 1]


def _plan_kernel(eid_ref, dest_ref, meta_ref, cnt_scr, carry_scr):
    ph = pl.program_id(0)
    i = pl.program_id(1)
    ts = eid_ref.shape[0]
    eid = eid_ref[...]
    o0 = _onehot(eid, 0)
    o1 = _onehot(eid, 1)
    osum = jnp.where(o0, 1.0, 0.0) + jnp.where(o1, 1.0, 0.0)
    tile_cnt = jnp.sum(osum, axis=0, keepdims=True)

    @pl.when((ph == 0) & (i == 0))
    def _():
        cnt_scr[...] = jnp.zeros(cnt_scr.shape, F32)

    @pl.when(ph == 0)
    def _():
        cnt_scr[...] = cnt_scr[...] + tile_cnt

    @pl.when(ph == 1)
    def _():
        cnt = cnt_scr[...]
        nblk = jnp.floor((cnt + (MOE_ROWS - 1)) * (1.0 / MOE_ROWS))
        er = lax.broadcasted_iota(jnp.int32, (LANES, LANES), 0)
        ec = lax.broadcasted_iota(jnp.int32, (LANES, LANES), 1)
        before = jnp.where(er < ec, 1.0, 0.0).astype(BF16)
        bstart = jnp.dot(nblk.astype(BF16), before, preferred_element_type=F32)
        row_start = bstart * MOE_ROWS

        @pl.when(i == 0)
        def _():
            carry_scr[...] = jnp.zeros(carry_scr.shape, F32)
            bend = bstart + nblk
            bend_col = jnp.sum(jnp.where(er == ec, bend[0:1], 0.0), axis=1, keepdims=True)
            e2 = lax.broadcasted_iota(jnp.int32, (LANES, META_LANES), 0)
            b2 = lax.broadcasted_iota(jnp.int32, (LANES, META_LANES), 1).astype(F32)
            done = jnp.where((bend_col <= b2) & (e2 < N_EXPERTS), 1.0, 0.0)
            block_e = jnp.minimum(jnp.sum(done, axis=0, keepdims=True), N_EXPERTS - 1.0)
            n_used = jnp.sum(nblk[0:1], axis=1, keepdims=True)
            zpad = jnp.zeros((1, META_LANES - LANES), F32)
            fill_start = jnp.concatenate([row_start[0:1] + cnt[0:1], zpad], axis=1)
            fill_count = jnp.concatenate([nblk[0:1] * MOE_ROWS - cnt[0:1], zpad], axis=1)
            row = lax.broadcasted_iota(jnp.int32, meta_ref.shape, 0)
            meta = jnp.where(row == META_BLOCK_E, block_e,
                             jnp.where(row == META_N_USED, n_used,
                                       jnp.where(row == META_FILL_START, fill_start,
                                                 jnp.where(row == META_FILL_COUNT, fill_count, 0.0))))
            meta_ref[...] = meta.astype(jnp.int32)

        rr = lax.broadcasted_iota(jnp.int32, (ts, ts), 0)
        rc = lax.broadcasted_iota(jnp.int32, (ts, ts), 1)
        earlier = jnp.where(rc < rr, 1.0, 0.0).astype(BF16)
        base = (jnp.dot(earlier, osum.astype(BF16), preferred_element_type=F32)
                + carry_scr[0:1] + row_start[0:1])
        d0 = jnp.sum(jnp.where(o0, base, 0.0), axis=1, keepdims=True)
        d1 = jnp.sum(jnp.where(o1, base + jnp.where(o0, 1.0, 0.0), 0.0), axis=1, keepdims=True)
        lane = lax.broadcasted_iota(jnp.int32, (ts, LANES), 1)
        dest_ref[...] = jnp.where(lane == 0, d0, jnp.where(lane == 1, d1, 0.0)).astype(jnp.int32)
        carry_scr[...] = carry_scr[...] + tile_cnt


def _plan(eid):
    t = eid.shape[0]
    ts = ROW_TILE
    dest, meta = pl.pallas_call(
        _plan_kernel, grid=(2, t // ts),
        in_specs=[pl.BlockSpec((ts, LANES), lambda ph, i: (i, 0))],
        out_specs=[pl.BlockSpec((ts, LANES), lambda ph, i: (ph * i, 0)),
                   pl.BlockSpec((8, META_LANES), lambda ph, i: (0, 0))],
        out_shape=[jax.ShapeDtypeStruct((t, LANES), jnp.int32),
                   jax.ShapeDtypeStruct((8, META_LANES), jnp.int32)],
        scratch_shapes=[pltpu.VMEM((8, LANES), F32), pltpu.VMEM((8, LANES), F32)],
        compiler_params=_params("arbitrary", "arbitrary"), name="moe_plan",
    )(eid)
    dest_rows = dest[:, :TOP_K].reshape(t // ts, 1, TOP_K * ts)
    return dest_rows, meta


def _dispatch_kernel(meta_ref, dest_ref, x_ref, xr_hbm, zrow, zblk, zsem, sem):
    i = pl.program_id(0)
    ts = dest_ref.shape[1] // TOP_K

    @pl.when(i == 0)
    def _():
        zrow[...] = jnp.zeros(zrow.shape, F32)
        for e in range(N_EXPERTS):
            start = meta_ref[META_FILL_START, e]
            count = meta_ref[META_FILL_COUNT, e]

            def fill(j, c, start=start):
                pltpu.make_async_copy(zrow, xr_hbm.at[pl.ds(start + j, 1)], zsem).start()
                return c

            def drain(j, c):
                pltpu.make_async_copy(zrow, xr_hbm.at[pl.ds(0, 1)], zsem).wait()
                return c

            lax.fori_loop(0, count, fill, 0)
            lax.fori_loop(0, count, drain, 0)

        zblk[...] = jnp.zeros(zblk.shape, F32)
        n_used = meta_ref[META_N_USED, 0]
        n_blocks = xr_hbm.shape[0] // MOE_ROWS

        def tail(b, c):
            cp = pltpu.make_async_copy(zblk, xr_hbm.at[pl.ds(pl.multiple_of(b * MOE_ROWS, MOE_ROWS), MOE_ROWS)], zsem)
            cp.start()
            cp.wait()
            return c

        lax.fori_loop(n_used, n_blocks, tail, 0)

    def send(t, c):
        src = x_ref.at[pl.ds(t, 1)]
        for k in range(TOP_K):
            pltpu.make_async_copy(src, xr_hbm.at[pl.ds(dest_ref[0, TOP_K * t + k], 1)], sem).start()
        return c

    def done(t, c):
        for k in range(TOP_K):
            pltpu.make_async_copy(x_ref.at[pl.ds(0, 1)], xr_hbm.at[pl.ds(0, 1)], sem).wait()
        return c

    lax.fori_loop(0, ts, send, 0, unroll=8)
    lax.fori_loop(0, ts, done, 0, unroll=8)


def _dispatch(xn, dest_rows, meta, n_rows):
    t, d = xn.shape
    nt = dest_rows.shape[0]
    grid_spec = pltpu.PrefetchScalarGridSpec(
        num_scalar_prefetch=1, grid=(nt,),
        in_specs=[pl.BlockSpec((None, 1, dest_rows.shape[2]), lambda i, m: (i, 0, 0),
                               memory_space=pltpu.SMEM),
                  pl.BlockSpec((dest_rows.shape[2] // TOP_K, d), lambda i, m: (i, 0))],
        out_specs=pl.BlockSpec(memory_space=pl.ANY),
        scratch_shapes=[pltpu.VMEM((1, d), F32), pltpu.VMEM((MOE_ROWS, d), F32),
                        pltpu.SemaphoreType.DMA(()), pltpu.SemaphoreType.DMA(())])
    return pl.pallas_call(
        _dispatch_kernel, grid_spec=grid_spec,
        out_shape=jax.ShapeDtypeStruct((n_rows, d), F32),
        compiler_params=_params("arbitrary"), name="moe_dispatch",
    )(meta, dest_rows, xn)


def _expert_kernel(meta_ref, x_ref, wgf_ref, wuf_ref, wdf_ref, y_ref, wg_ref, wu_ref, wd_ref):
    blk = pl.program_id(0)
    active = blk < meta_ref[META_N_USED, 0]
    new_expert = (blk == 0) | (meta_ref[META_BLOCK_E, blk] != meta_ref[META_BLOCK_E, jnp.maximum(blk - 1, 0)])

    @pl.when(active & new_expert)
    def _():
        wg_ref[...] = wgf_ref[...].astype(BF16)
        wu_ref[...] = wuf_ref[...].astype(BF16)
        wd_ref[...] = wdf_ref[...].astype(BF16)

    @pl.when(active)
    def _():
        x = x_ref[...].astype(BF16)
        gate = jnp.dot(x, wg_ref[...], preferred_element_type=F32)
        up = jnp.dot(x, wu_ref[...], preferred_element_type=F32)
        mid = (gate * jax.nn.sigmoid(gate)) * up
        y_ref[...] = jnp.dot(mid.astype(BF16), wd_ref[...], preferred_element_type=F32)

    @pl.when(blk >= meta_ref[META_N_USED, 0])
    def _():
        y_ref[...] = jnp.zeros(y_ref.shape, F32)


def _experts(x_rows, meta, wg, wu, wd, layer):
    n_rows, d = x_rows.shape
    ff = wg.shape[-1]
    wspec = lambda shape: pl.BlockSpec((None, None) + shape,
                                       lambda i, m: (layer, m[META_BLOCK_E, i], 0, 0))
    grid_spec = pltpu.PrefetchScalarGridSpec(
        num_scalar_prefetch=1, grid=(n_rows // MOE_ROWS,),
        in_specs=[pl.BlockSpec((MOE_ROWS, d), lambda i, m: (jnp.minimum(i, m[META_N_USED, 0] - 1), 0)),
                  wspec((d, ff)), wspec((d, ff)), wspec((ff, d))],
        out_specs=pl.BlockSpec((MOE_ROWS, d), lambda i, m: (i, 0)),
        scratch_shapes=[pltpu.VMEM((d, ff), BF16), pltpu.VMEM((d, ff), BF16), pltpu.VMEM((ff, d), BF16)])
    return pl.pallas_call(
        _expert_kernel, grid_spec=grid_spec,
        out_shape=jax.ShapeDtypeStruct((n_rows, d), F32),
        compiler_params=_params("arbitrary"), name="experts",
    )(meta, x_rows, wg, wu, wd)


def _combine_kernel(dest_ref, h_ref, gate_ref, g_ref, y_hbm, o_ref, y0buf, y1buf, sem, *, final):
    ts = h_ref.shape[0]
    bufs = (y0buf, y1buf)

    def fetch(t, c):
        for k in range(TOP_K):
            pltpu.make_async_copy(y_hbm.at[pl.ds(dest_ref[0, TOP_K * t + k], 1)],
                                  bufs[k].at[pl.ds(t, 1)], sem).start()
        return c

    def done(t, c):
        for k in range(TOP_K):
            pltpu.make_async_copy(y_hbm.at[pl.ds(0, 1)], bufs[k].at[pl.ds(0, 1)], sem).wait()
        return c

    lax.fori_loop(0, ts, fetch, 0, unroll=8)
    lax.fori_loop(0, ts, done, 0, unroll=8)
    gate = gate_ref[...]
    out = h_ref[...] + gate[:, 0:1] * y0buf[...] + gate[:, 1:2] * y1buf[...]
    o_ref[...] = _rms(out, g_ref[...]) if final else out


def _combine(h2, y_rows, dest_rows, gate, g, final):
    t, d = h2.shape
    nt = dest_rows.shape[0]
    ts = t // nt
    assert TOP_K == 2
    return pl.pallas_call(
        functools.partial(_combine_kernel, final=final), grid=(nt,),
        in_specs=[pl.BlockSpec((None, 1, TOP_K * ts), lambda i: (i, 0, 0), memory_space=pltpu.SMEM),
                  pl.BlockSpec((ts, d), lambda i: (i, 0)),
                  pl.BlockSpec((ts, LANES), lambda i: (i, 0)),
                  pl.BlockSpec((1, d), lambda i: (0, 0)),
                  pl.BlockSpec(memory_space=pl.ANY)],
        out_specs=pl.BlockSpec((ts, d), lambda i: (i, 0)),
        out_shape=jax.ShapeDtypeStruct(h2.shape, F32),
        scratch_shapes=[pltpu.VMEM((ts, d), F32), pltpu.VMEM((ts, d), F32), pltpu.SemaphoreType.DMA(())],
        compiler_params=_params("arbitrary"), name="moe_combine",
    )(dest_rows, h2, gate, g.reshape(1, d), y_rows)


def _moe(h, g, rg_w, rg_b, re_w, re_b, wg, wu, wd, layer, final_g, final):
    b, s, d = h.shape
    t = b * s
    h2 = h.reshape(t, d)
    xn, eid, gate = _router(h2, g, rg_w, rg_b, re_w, re_b)
    dest_rows, meta = _plan(eid)
    n_rows = -(-(t * TOP_K + N_EXPERTS * (MOE_ROWS - 1)) // MOE_ROWS) * MOE_ROWS
    assert n_rows // MOE_ROWS <= META_LANES
    x_rows = _dispatch(xn, dest_rows, meta, n_rows)
    y_rows = _experts(x_rows, meta, wg, wu, wd, layer)
    return _combine(h2, y_rows, dest_rows, gate, final_g, final).reshape(b, s, d)


def _kv_kernel(h_ref, gin_ref, wdkvf_ref, glat_ref, wkrf_ref, wukf_ref, wuvf_ref, cs_ref, sn_ref,
               k_ref, v_ref, wdkv_ref, wkr_ref, wuk_ref, wuv_ref):
    _cast_weights_once((pl.program_id(0) == 0) & (pl.program_id(1) == 0),
                       [(wdkvf_ref, wdkv_ref), (wkrf_ref, wkr_ref), (wukf_ref, wuk_ref),
                        (wuvf_ref, wuv_ref)])
    hn = _rms(h_ref[...], gin_ref[...])
    ckv = _rms(_dot(hn, wdkv_ref[...]), glat_ref[...])
    kr = _dot(hn, wkr_ref[...])
    kr = kr * cs_ref[...] + _swap_halves(kr) * sn_ref[...]
    lane = lax.broadcasted_iota(jnp.int32, kr.shape, 1)
    kr = jnp.where(lane < ROPE_DIM, kr, 0.0).astype(BF16)
    kn = _dot(ckv, wuk_ref[...]).astype(BF16)
    vv = _dot(ckv, wuv_ref[...])
    for hd in range(MLA_HEADS):
        k_ref[hd, :, 0:NOPE_DIM] = kn[:, hd * NOPE_DIM:(hd + 1) * NOPE_DIM]
        k_ref[hd, :, NOPE_DIM:QK_PAD] = kr
        v_ref[hd] = vv[:, hd * V_DIM:(hd + 1) * V_DIM].T.astype(BF16)


def _shared_kv(h, cs, sn, gin, wdkv, glat, wkr, wuk, wuv):
    b, s, d = h.shape
    ts = ROW_TILE
    r = wdkv.shape[1]
    wkr_p = jnp.concatenate([wkr, jnp.zeros((d, LANES - ROPE_DIM), F32)], axis=1)
    full = lambda shape: pl.BlockSpec(shape, lambda i, j: (0,) * len(shape))
    tab = pl.BlockSpec((None, ts, LANES), lambda i, j: (i, j, 0))
    wshapes = [(d, r), (d, LANES), (r, MLA_HEADS * NOPE_DIM), (r, MLA_HEADS * V_DIM)]
    return pl.pallas_call(
        _kv_kernel, grid=(b, s // ts),
        in_specs=[pl.BlockSpec((None, ts, d), lambda i, j: (i, j, 0)),
                  full((1, d)), _weight_spec(wshapes[0]), full((1, r)), _weight_spec(wshapes[1]),
                  _weight_spec(wshapes[2]), _weight_spec(wshapes[3]), tab, tab],
        out_specs=[pl.BlockSpec((None, MLA_HEADS, ts, QK_PAD), lambda i, j: (i, 0, j, 0)),
                   pl.BlockSpec((None, MLA_HEADS, V_DIM, ts), lambda i, j: (i, 0, 0, j))],
        out_shape=[jax.ShapeDtypeStruct((b, MLA_HEADS, s, QK_PAD), BF16),
                   jax.ShapeDtypeStruct((b, MLA_HEADS, V_DIM, s), BF16)],
        scratch_shapes=[pltpu.VMEM(ws, BF16) for ws in wshapes],
        compiler_params=_params("arbitrary", "arbitrary"), name="mla_shared_kv",
    )(h, gin.reshape(1, d), wdkv, glat.reshape(1, r), wkr_p, wuk, wuv, cs, sn)


def _q_kernel(h_ref, g_ref, wdqf_ref, glat_ref, wuqf_ref, wqrf_ref, cs_ref, sn_ref, q_ref,
              wdq_ref, wuq_ref, wqr_ref):
    _cast_weights_once((pl.program_id(0) == 0) & (pl.program_id(1) == 0),
                       [(wdqf_ref, wdq_ref), (wuqf_ref, wuq_ref), (wqrf_ref, wqr_ref)])
    xn = _rms(h_ref[...], g_ref[...])
    cq = _rms(_dot(xn, wdq_ref[...]), glat_ref[...])
    scale = (NOPE_DIM + ROPE_DIM) ** -0.5 * LOG2_E
    qn = (_dot(cq, wuq_ref[...]) * scale).astype(BF16)
    qr = _dot(cq, wqr_ref[...]) * scale
    cs = cs_ref[...]
    sn = sn_ref[...]
    lane = lax.broadcasted_iota(jnp.int32, cs.shape, 1)
    for pair in range(MLA_HEADS // 2):
        x = qr[:, pair * LANES:(pair + 1) * LANES]
        rot = x * cs + _swap_halves(x) * sn
        for odd in range(2):
            hd = 2 * pair + odd
            part = pltpu.roll(rot, ROPE_DIM, 1) if odd else rot
            q_ref[hd, :, 0:NOPE_DIM] = qn[:, hd * NOPE_DIM:(hd + 1) * NOPE_DIM]
            q_ref[hd, :, NOPE_DIM:QK_PAD] = jnp.where(lane < ROPE_DIM, part, 0.0).astype(BF16)


def _q_proj(h, cs, sn, g, wdq, glat, wuq, wqr, layer):
    b, s, d = h.shape
    ts = ROW_TILE
    r = wdq.shape[-1]
    full = lambda shape: pl.BlockSpec(shape, lambda i, j: (0,) * len(shape))
    tab = pl.BlockSpec((None, ts, LANES), lambda i, j: (i, j, 0))
    wshapes = [(d, r), (r, MLA_HEADS * NOPE_DIM), (r, MLA_HEADS * ROPE_DIM)]
    return pl.pallas_call(
        _q_kernel, grid=(b, s // ts),
        in_specs=[pl.BlockSpec((None, ts, d), lambda i, j: (i, j, 0)),
                  full((1, d)), _weight_spec(wshapes[0], layer), full((1, r)),
                  _weight_spec(wshapes[1], layer), _weight_spec(wshapes[2], layer), tab, tab],
        out_specs=pl.BlockSpec((None, MLA_HEADS, ts, QK_PAD), lambda i, j: (i, 0, j, 0)),
        out_shape=jax.ShapeDtypeStruct((b, MLA_HEADS, s, QK_PAD), BF16),
        scratch_shapes=[pltpu.VMEM(ws, BF16) for ws in wshapes],
        compiler_params=_params("arbitrary", "arbitrary"), name="mla_q_proj",
    )(h, g.reshape(1, d), wdq, glat.reshape(1, r), wuq, wqr, cs, sn)


def _flash_kernel(q_ref, k_ref, v_ref, o_ref, m_scr, l_scr, acc_scr, s_scr):
    qi = pl.program_id(1)
    ki = pl.program_id(2)
    tq = q_ref.shape[1]
    tk = k_ref.shape[1]

    @pl.when(ki == 0)
    def _():
        m_scr[...] = jnp.full(m_scr.shape, -jnp.inf, F32)
        l_scr[...] = jnp.zeros(l_scr.shape, F32)
        acc_scr[...] = jnp.zeros(acc_scr.shape, F32)

    def scores(hd):
        s_scr[hd % 2] = lax.dot_general(k_ref[hd], q_ref[hd], (((1,), (1,)), ((), ())),
                                        preferred_element_type=F32)

    def softmax_pv(hd, bias):
        s = s_scr[hd % 2]
        if bias is not None:
            s = s + bias
        m_old = m_scr[hd]
        m_new = jnp.maximum(m_old, jnp.max(s, axis=0, keepdims=True))
        alpha = jnp.exp2(m_old - m_new)
        p = jnp.exp2(s - m_new[0:1])
        l_scr[hd] = alpha * l_scr[hd] + jnp.sum(p, axis=0, keepdims=True)
        acc_scr[hd] = alpha[0:1] * acc_scr[hd] + jnp.dot(v_ref[hd], p.astype(BF16),
                                                         preferred_element_type=F32)
        m_scr[hd] = m_new

    def all_heads(bias):
        scores(0)
        for hd in range(MLA_HEADS):
            if hd + 1 < MLA_HEADS:
                scores(hd + 1)
            softmax_pv(hd, bias)

    @pl.when(ki < qi)
    def _():
        all_heads(None)

    @pl.when(ki == qi)
    def _():
        keyc = lax.broadcasted_iota(jnp.int32, (tk, tq), 0) // CHUNK
        qryc = lax.broadcasted_iota(jnp.int32, (tk, tq), 1) // CHUNK
        all_heads(jnp.where(keyc <= qryc, 0.0, -jnp.inf))
        for hd in range(MLA_HEADS):
            o_t = acc_scr[hd] / l_scr[hd][0:1]
            o_ref[:, hd * V_DIM:(hd + 1) * V_DIM] = o_t.T.astype(BF16)


def _flash(q, k, v):
    b, nh, s, _ = q.shape
    tq, tk = ATT_TQ, ATT_TK
    assert tq == tk and tq % CHUNK == 0
    return pl.pallas_call(
        _flash_kernel, grid=(b, s // tq, s // tk),
        in_specs=[pl.BlockSpec((None, nh, tq, QK_PAD), lambda i, qi, ki: (i, 0, qi, 0)),
                  pl.BlockSpec((None, nh, tk, QK_PAD), lambda i, qi, ki: (i, 0, jnp.minimum(ki, qi), 0)),
                  pl.BlockSpec((None, nh, V_DIM, tk), lambda i, qi, ki: (i, 0, 0, jnp.minimum(ki, qi)))],
        out_specs=pl.BlockSpec((None, tq, nh * V_DIM), lambda i, qi, ki: (i, qi, 0)),
        out_shape=jax.ShapeDtypeStruct((b, s, nh * V_DIM), BF16),
        scratch_shapes=[pltpu.VMEM((nh, 8, tq), F32), pltpu.VMEM((nh, 8, tq), F32),
                        pltpu.VMEM((nh, V_DIM, tq), F32), pltpu.VMEM((2, tk, tq), F32)],
        compiler_params=_params("parallel", "parallel", "arbitrary"), name="mla_flash",
    )(q, k, v)


def _oproj_kernel(h_ref, o_ref, wf_ref, out_ref, w_ref):
    _cast_weights_once((pl.program_id(0) == 0) & (pl.program_id(1) == 0), [(wf_ref, w_ref)])
    out_ref[...] = h_ref[...] + jnp.dot(o_ref[...], w_ref[...], preferred_element_type=F32)


def _o_proj(h, o, w_all, layer):
    b, s, d = h.shape
    ts = ROW_TILE
    tile = lambda n: pl.BlockSpec((None, ts, n), lambda i, j: (i, j, 0))
    wshape = w_all.shape[1:]
    return pl.pallas_call(
        _oproj_kernel, grid=(b, s // ts),
        in_specs=[tile(d), tile(o.shape[-1]), _weight_spec(wshape, layer)],
        out_specs=tile(d), out_shape=jax.ShapeDtypeStruct(h.shape, F32),
        scratch_shapes=[pltpu.VMEM(wshape, BF16)],
        compiler_params=_params("arbitrary", "arbitrary"), name="mla_o_proj",
    )(h, o, w_all)


def kernel(x, mem, positions, norm_mix, norm_mem, norm_memtok, norm_ffn, pool_w, pool_scale, kv_in_norm, w_dkv, kv_latent_norm, w_kr, w_uk, w_uv, w_dq, q_latent_norm, w_uq, w_qr, w_o, mem_wq, mem_wk, mem_wv, mem_wo, rg_w, rg_b, re_w, re_b, w_gate, w_up, w_down, final_norm):
    cs, sn = _rope_tables(positions)
    mk, mv = _mem_kv(mem, norm_memtok, mem_wk, mem_wv)
    h = x
    k_sh = v_sh = None
    for l in range(DEPTH):
        if l < N_A_LAYERS:
            h = _pool_layer(h, norm_mix[l], pool_w, pool_scale[l], l)
        else:
            if l == N_A_LAYERS:
                k_sh, v_sh = _shared_kv(h, cs, sn, kv_in_norm, w_dkv, kv_latent_norm, w_kr, w_uk, w_uv)
            j = l - N_A_LAYERS
            q = _q_proj(h, cs, sn, norm_mix[l], w_dq, q_latent_norm[j], w_uq, w_qr, j)
            h = _o_proj(h, _flash(q, k_sh, v_sh), w_o, j)
        h = _mem_attn(h, norm_mem[l], mem_wq, mk, mv, mem_wo, l)
        h = _moe(h, norm_ffn[l], rg_w[l], rg_b[l], re_w[l], re_b[l], w_gate, w_up, w_down, l,
                 final_norm, l == DEPTH - 1)
    return h
```

```python
import functools

import jax
import jax.numpy as jnp
from jax import lax
from jax.experimental import pallas as pl
from jax.experimental.pallas import tpu as pltpu

D_MODEL = 2048
DEPTH = 4
CHUNK = 64
N_A_LAYERS = DEPTH // 2
RMS_EPS = 1e-6
POOL_WINDOWS = (2, 4, 8, 16)
POOL_GROUP = D_MODEL // len(POOL_WINDOWS)
MLA_HEADS = 16
NOPE_DIM = 128
ROPE_DIM = 64
V_DIM = 128
ROPE_THETA = 10000.0
MEM_HEADS = 4
MEM_HEAD_DIM = 128
MEM_INNER = MEM_HEADS * MEM_HEAD_DIM
N_GROUPS = 4
EXPERTS_PER_GROUP = 8
N_EXPERTS = N_GROUPS * EXPERTS_PER_GROUP
TOP_K = 2
EXPERT_FF = D_MODEL // 4

LANES = 128
QK_PAD = 256
HALO = 16
POOL_SUB = 256
MOE_ROWS = 256
ROW_TILE = 512
ATT_TQ = 512
ATT_TK = 512
LOG2_E = 1.4426950408889634
VMEM_LIMIT = 56 * 1024 * 1024

F32 = jnp.float32
BF16 = jnp.bfloat16


def _params(*sem):
    return pltpu.CompilerParams(dimension_semantics=sem, vmem_limit_bytes=VMEM_LIMIT)


def _rms(x, g):
    return x * lax.rsqrt(jnp.mean(x * x, axis=-1, keepdims=True) + RMS_EPS) * g


def _dot(a, b):
    return jnp.dot(a.astype(BF16), b.astype(BF16), preferred_element_type=F32)


def _dot_t(a, b):
    return lax.dot_general(a.astype(BF16), b.astype(BF16), (((1,), (1,)), ((), ())),
                           preferred_element_type=F32)


def _pack_halves(x):
    n = x.shape[1] // 2
    lo = lax.bitcast_convert_type(x[:, :n].astype(BF16).astype(F32), jnp.uint32)
    hi = lax.bitcast_convert_type(x[:, n:].astype(BF16).astype(F32), jnp.uint32)
    return lax.shift_right_logical(lo, jnp.uint32(16)) | (hi & jnp.uint32(0xFFFF0000))


def _unpack_halves(u):
    lo = lax.bitcast_convert_type(lax.shift_left(u, jnp.uint32(16)), F32)
    hi = lax.bitcast_convert_type(u & jnp.uint32(0xFFFF0000), F32)
    return lo, hi


def _weight_spec(shape, layer=None):
    if layer is None:
        return pl.BlockSpec(shape, lambda *_: (0,) * len(shape), pipeline_mode=pl.Buffered(1))
    return pl.BlockSpec((None,) + shape, lambda *_: (layer,) + (0,) * len(shape),
                        pipeline_mode=pl.Buffered(1))


def _cast_weights_once(first, pairs):
    @pl.when(first)
    def _():
        for src, dst in pairs:
            dst[...] = src[...].astype(BF16)


def _swap_halves(x):
    lane = lax.broadcasted_iota(jnp.int32, x.shape, 1)
    return jnp.where((lane & 32) == 0, pltpu.roll(x, 96, 1), pltpu.roll(x, 32, 1))


def _rope_kernel(pos_ref, cs_ref, sn_ref):
    pos = pos_ref[...].astype(F32)
    lane = lax.broadcasted_iota(jnp.int32, (1, LANES), 1)
    half = ROPE_DIM // 2
    frac = (lane % half).astype(F32) / half
    inv = 1.0 / jnp.power(jnp.full((1, LANES), ROPE_THETA, F32), frac)
    ang = pos * inv
    sign = jnp.where((lane & half) == 0, -1.0, 1.0)
    cs_ref[...] = jnp.cos(ang)
    sn_ref[...] = jnp.sin(ang) * sign


def _rope_tables(positions):
    b, s = positions.shape
    ts = ROW_TILE
    out = jax.ShapeDtypeStruct((b, s, LANES), F32)
    spec = pl.BlockSpec((None, ts, LANES), lambda i, j: (i, j, 0))
    return pl.pallas_call(
        _rope_kernel, grid=(b, s // ts),
        in_specs=[pl.BlockSpec((None, ts, 1), lambda i, j: (i, j, 0))],
        out_specs=[spec, spec], out_shape=[out, out],
        compiler_params=_params("parallel", "parallel"), name="rope_tables",
    )(positions.reshape(b, s, 1))


def _pool_kernel(h_ref, halo_ref, g_ref, wf_ref, sc_ref, o_ref, buf_ref, w_ref):
    ts = h_ref.shape[0]
    i = pl.program_id(1)
    _cast_weights_once((pl.program_id(0) == 0) & (i == 0), [(wf_ref, w_ref)])
    g = g_ref[...]
    xn = _rms(h_ref[...], g)
    halo = _rms(halo_ref[...], g)
    buf_ref[0:HALO, :] = jnp.where(i > 0, halo, 0.0)
    buf_ref[HALO:HALO + ts, :] = xn
    r = POOL_SUB
    row = lax.broadcasted_iota(jnp.int32, (r, r + HALO), 0)
    col = lax.broadcasted_iota(jnp.int32, (r, r + HALO), 1)
    off = col - row
    for sub in range(ts // r):
        ext = buf_ref[sub * r:sub * r + r + HALO, :]
        ext_hi = ext.astype(BF16)
        ext_lo = (ext - ext_hi.astype(F32)).astype(BF16)
        t_seq = i * ts + sub * r + lax.broadcasted_iota(jnp.int32, (r, 1), 0)
        for gi, w in enumerate(POOL_WINDOWS):
            cols = slice(gi * POOL_GROUP, (gi + 1) * POOL_GROUP)
            band = jnp.where((off > HALO - w) & (off <= HALO), 1.0, 0.0).astype(BF16)
            wsum = (jnp.dot(band, ext_hi[:, cols], preferred_element_type=F32)
                    + jnp.dot(band, ext_lo[:, cols], preferred_element_type=F32))
            cnt = jnp.minimum(t_seq + 1, w).astype(F32)
            u = wsum / cnt - ext[HALO:, cols]
            y = _dot(u, w_ref[gi])
            rows = slice(sub * r, (sub + 1) * r)
            o_ref[rows, cols] = h_ref[rows, cols] + y * sc_ref[:, cols]


def _pool_layer(h, g, w_all, sc, layer):
    b, s, d = h.shape
    ts = ROW_TILE
    hpt = ts // HALO
    wshape = w_all.shape[1:]
    return pl.pallas_call(
        _pool_kernel, grid=(b, s // ts),
        in_specs=[
            pl.BlockSpec((None, ts, d), lambda i, j: (i, j, 0)),
            pl.BlockSpec((None, HALO, d), lambda i, j: (i, jnp.maximum(j * hpt - 1, 0), 0)),
            pl.BlockSpec((1, d), lambda i, j: (0, 0)),
            _weight_spec(wshape, layer),
            pl.BlockSpec((1, d), lambda i, j: (0, 0)),
        ],
        out_specs=pl.BlockSpec((None, ts, d), lambda i, j: (i, j, 0)),
        out_shape=jax.ShapeDtypeStruct(h.shape, F32),
        scratch_shapes=[pltpu.VMEM((ts + HALO, d), F32), pltpu.VMEM(wshape, BF16)],
        compiler_params=_params("arbitrary", "arbitrary"), name="pool_layer",
    )(h, h, g.reshape(1, d), w_all, sc.reshape(1, d))


def _mem_kv_kernel(m_ref, g_ref, wk_ref, wv_ref, k_ref, v_ref):
    m = _rms(m_ref[...], g_ref[...])
    k_ref[...] = _dot(m, wk_ref[...]).astype(BF16)
    v_ref[...] = _dot(m, wv_ref[...]).astype(BF16)


def _mem_kv(mem, g, wk, wv):
    b, m, d = mem.shape
    nl = g.shape[0]
    out = jax.ShapeDtypeStruct((nl, b, m, MEM_INNER), BF16)
    ospec = pl.BlockSpec((None, None, m, MEM_INNER), lambda l, i: (l, i, 0, 0))
    wspec = pl.BlockSpec((None, d, MEM_INNER), lambda l, i: (l, 0, 0))
    return pl.pallas_call(
        _mem_kv_kernel, grid=(nl, b),
        in_specs=[pl.BlockSpec((None, m, d), lambda l, i: (i, 0, 0)),
                  pl.BlockSpec((None, 1, d), lambda l, i: (l, 0, 0)), wspec, wspec],
        out_specs=[ospec, ospec], out_shape=[out, out],
        compiler_params=_params("arbitrary", "arbitrary"), name="mem_kv",
    )(mem, g.reshape(nl, 1, d), wk, wv)


def _mem_attn_kernel(h_ref, g_ref, wqf_ref, k_ref, v_ref, wof_ref, o_ref, wq_ref, wo_ref):
    _cast_weights_once((pl.program_id(0) == 0) & (pl.program_id(1) == 0),
                       [(wqf_ref, wq_ref), (wof_ref, wo_ref)])
    h = h_ref[...]
    q = _dot(_rms(h, g_ref[...]), wq_ref[...]).astype(BF16)
    scale = MEM_HEAD_DIM ** -0.5
    heads = []
    for hd in range(MEM_HEADS):
        cols = slice(hd * MEM_HEAD_DIM, (hd + 1) * MEM_HEAD_DIM)
        sc = _dot_t(q[:, cols], k_ref[:, cols]) * scale
        p = jnp.exp(sc - jnp.max(sc, axis=-1, keepdims=True))
        den = jnp.sum(p, axis=-1, keepdims=True)
        heads.append(_dot(p, v_ref[:, cols]) / den)
    o = jnp.concatenate(heads, axis=-1)
    o_ref[...] = h + _dot(o, wo_ref[...])


def _mem_attn(h, g, wq, k, v, wo, layer):
    b, s, d = h.shape
    ts = ROW_TILE
    m = k.shape[2]
    kvspec = pl.BlockSpec((None, None, m, MEM_INNER), lambda i, j: (layer, i, 0, 0))
    return pl.pallas_call(
        _mem_attn_kernel, grid=(b, s // ts),
        in_specs=[pl.BlockSpec((None, ts, d), lambda i, j: (i, j, 0)),
                  pl.BlockSpec((1, d), lambda i, j: (0, 0)),
                  _weight_spec((d, MEM_INNER), layer),
                  kvspec, kvspec,
                  _weight_spec((MEM_INNER, d), layer)],
        out_specs=pl.BlockSpec((None, ts, d), lambda i, j: (i, j, 0)),
        out_shape=jax.ShapeDtypeStruct(h.shape, F32),
        scratch_shapes=[pltpu.VMEM((d, MEM_INNER), BF16), pltpu.VMEM((MEM_INNER, d), BF16)],
        compiler_params=_params("arbitrary", "arbitrary"), name="mem_attn",
    )(h, g.reshape(1, d), wq, k, v, wo)


def _router_kernel(h_ref, g_ref, w2_ref, wh_ref, b_ref, xn_ref, eid_ref, gate_ref):
    xn = _rms(h_ref[...], g_ref[...])
    xn_ref[...] = _pack_halves(xn)
    x_hi = xn.astype(BF16)
    x_lo = (xn - x_hi.astype(F32)).astype(BF16)
    a = jnp.dot(x_hi, w2_ref[...], preferred_element_type=F32)
    c = jnp.dot(x_lo, wh_ref[...], preferred_element_type=F32)
    logits = a[:, :LANES] + a[:, LANES:] + c + b_ref[...]
    lane = lax.broadcasted_iota(jnp.int32, logits.shape, 1)
    neg = -jnp.inf
    gl = jnp.where(lane < N_GROUPS, logits, neg)
    gmax = jnp.max(gl, axis=-1, keepdims=True)
    g_sel = jnp.min(jnp.where(gl == gmax, lane, LANES), axis=-1, keepdims=True)
    g_p = 1.0 / jnp.sum(jnp.exp(gl - gmax), axis=-1, keepdims=True)
    e_idx = lane - N_GROUPS
    in_grp = (e_idx >= 0) & (e_idx < N_EXPERTS) & ((e_idx >> 3) == g_sel)
    assert EXPERTS_PER_GROUP == 8
    el = jnp.where(in_grp, logits, neg)
    m1 = jnp.max(el, axis=-1, keepdims=True)
    i1 = jnp.min(jnp.where(el == m1, lane, LANES), axis=-1, keepdims=True)
    el2 = jnp.where(lane == i1, neg, el)
    m2 = jnp.max(el2, axis=-1, keepdims=True)
    i2 = jnp.min(jnp.where(el2 == m2, lane, LANES), axis=-1, keepdims=True)
    esum = jnp.sum(jnp.exp(el - m1), axis=-1, keepdims=True)
    p1 = 1.0 / esum
    p2 = jnp.exp(m2 - m1) / esum
    w1 = g_p * p1 / (p1 + p2)
    w2 = g_p * p2 / (p1 + p2)
    eid_ref[...] = jnp.where(lane == 0, i1 - N_GROUPS, jnp.where(lane == 1, i2 - N_GROUPS, 0))
    gate_ref[...] = jnp.where(lane == 0, w1, jnp.where(lane == 1, w2, 0.0))


def _router(h2, g, rg_w, rg_b, re_w, re_b):
    t, d = h2.shape
    ts = ROW_TILE
    pad = LANES - N_GROUPS - N_EXPERTS
    w = jnp.concatenate([rg_w, re_w, jnp.zeros((d, pad), F32)], axis=1)
    bias = jnp.concatenate([rg_b, re_b, jnp.zeros((pad,), F32)]).reshape(1, LANES)
    w_hi = w.astype(BF16)
    w_lo = (w - w_hi.astype(F32)).astype(BF16)
    w2 = jnp.concatenate([w_hi, w_lo], axis=1)
    row = lambda n, dt: jax.ShapeDtypeStruct((t, n), dt)
    return pl.pallas_call(
        _router_kernel, grid=(t // ts,),
        in_specs=[pl.BlockSpec((ts, d), lambda i: (i, 0)),
                  pl.BlockSpec((1, d), lambda i: (0, 0)),
                  pl.BlockSpec((d, 2 * LANES), lambda i: (0, 0)),
                  pl.BlockSpec((d, LANES), lambda i: (0, 0)),
                  pl.BlockSpec((1, LANES), lambda i: (0, 0))],
        out_specs=[pl.BlockSpec((ts, d // 2), lambda i: (i, 0)),
                   pl.BlockSpec((ts, LANES), lambda i: (i, 0)),
                   pl.BlockSpec((ts, LANES), lambda i: (i, 0))],
        out_shape=[row(d // 2, jnp.uint32), row(LANES, jnp.int32), row(LANES, F32)],
        compiler_params=_params("parallel"), name="router",
    )(h2, g.reshape(1, d), w2, w_hi, bias)


META_BLOCK_E, META_N_USED, META_FILL_START, META_FILL_COUNT = 0, 1, 2, 3
META_LANES = 2 * LANES


def _onehot(eid, k):
    lane = lax.broadcasted_iota(jnp.int32, eid.shape, 1)
    return lane == eid[:, k:k + 1]


def _plan_kernel(eid_ref, dest_ref, meta_ref, cnt_scr, carry_scr):
    ph = pl.program_id(0)
    i = pl.program_id(1)
    ts = eid_ref.shape[0]
    eid = eid_ref[...]
    o0 = _onehot(eid, 0)
    o1 = _onehot(eid, 1)
    osum = jnp.where(o0, 1.0, 0.0) + jnp.where(o1, 1.0, 0.0)
    tile_cnt = jnp.sum(osum, axis=0, keepdims=True)

    @pl.when((ph == 0) & (i == 0))
    def _():
        cnt_scr[...] = jnp.zeros(cnt_scr.shape, F32)

    @pl.when(ph == 0)
    def _():
        cnt_scr[...] = cnt_scr[...] + tile_cnt

    @pl.when(ph == 1)
    def _():
        cnt = cnt_scr[...]
        nblk = jnp.floor((cnt + (MOE_ROWS - 1)) * (1.0 / MOE_ROWS))
        er = lax.broadcasted_iota(jnp.int32, (LANES, LANES), 0)
        ec = lax.broadcasted_iota(jnp.int32, (LANES, LANES), 1)
        before = jnp.where(er < ec, 1.0, 0.0).astype(BF16)
        bstart = jnp.dot(nblk.astype(BF16), before, preferred_element_type=F32)
        row_start = bstart * MOE_ROWS

        @pl.when(i == 0)
        def _():
            carry_scr[...] = jnp.zeros(carry_scr.shape, F32)
            bend = bstart + nblk
            bend_col = jnp.sum(jnp.where(er == ec, bend[0:1], 0.0), axis=1, keepdims=True)
            e2 = lax.broadcasted_iota(jnp.int32, (LANES, META_LANES), 0)
            b2 = lax.broadcasted_iota(jnp.int32, (LANES, META_LANES), 1).astype(F32)
            done = jnp.where((bend_col <= b2) & (e2 < N_EXPERTS), 1.0, 0.0)
            block_e = jnp.minimum(jnp.sum(done, axis=0, keepdims=True), N_EXPERTS - 1.0)
            n_used = jnp.sum(nblk[0:1], axis=1, keepdims=True)
            zpad = jnp.zeros((1, META_LANES - LANES), F32)
            fill_start = jnp.concatenate([row_start[0:1] + cnt[0:1], zpad], axis=1)
            fill_count = jnp.concatenate([nblk[0:1] * MOE_ROWS - cnt[0:1], zpad], axis=1)
            row = lax.broadcasted_iota(jnp.int32, meta_ref.shape, 0)
            meta = jnp.where(row == META_BLOCK_E, block_e,
                             jnp.where(row == META_N_USED, n_used,
                                       jnp.where(row == META_FILL_START, fill_start,
                                                 jnp.where(row == META_FILL_COUNT, fill_count, 0.0))))
            meta_ref[...] = meta.astype(jnp.int32)

        rr = lax.broadcasted_iota(jnp.int32, (ts, ts), 0)
        rc = lax.broadcasted_iota(jnp.int32, (ts, ts), 1)
        earlier = jnp.where(rc < rr, 1.0, 0.0).astype(BF16)
        base = (jnp.dot(earlier, osum.astype(BF16), preferred_element_type=F32)
                + carry_scr[0:1] + row_start[0:1])
        d0 = jnp.sum(jnp.where(o0, base, 0.0), axis=1, keepdims=True)
        d1 = jnp.sum(jnp.where(o1, base + jnp.where(o0, 1.0, 0.0), 0.0), axis=1, keepdims=True)
        lane = lax.broadcasted_iota(jnp.int32, (ts, LANES), 1)
        dest_ref[...] = jnp.where(lane == 0, d0, jnp.where(lane == 1, d1, 0.0)).astype(jnp.int32)
        carry_scr[...] = carry_scr[...] + tile_cnt


def _plan(eid):
    t = eid.shape[0]
    ts = ROW_TILE
    dest, meta = pl.pallas_call(
        _plan_kernel, grid=(2, t // ts),
        in_specs=[pl.BlockSpec((ts, LANES), lambda ph, i: (i, 0))],
        out_specs=[pl.BlockSpec((ts, LANES), lambda ph, i: (ph * i, 0)),
                   pl.BlockSpec((8, META_LANES), lambda ph, i: (0, 0))],
        out_shape=[jax.ShapeDtypeStruct((t, LANES), jnp.int32),
                   jax.ShapeDtypeStruct((8, META_LANES), jnp.int32)],
        scratch_shapes=[pltpu.VMEM((8, LANES), F32), pltpu.VMEM((8, LANES), F32)],
        compiler_params=_params("arbitrary", "arbitrary"), name="moe_plan",
    )(eid)
    dest_rows = dest[:, :TOP_K].reshape(t // ts, 1, TOP_K * ts)
    return dest_rows, meta


def _dispatch_kernel(meta_ref, dest_ref, x_ref, xr_hbm, zrow, zblk, zsem, sem):
    i = pl.program_id(0)
    ts = dest_ref.shape[1] // TOP_K

    @pl.when(i == 0)
    def _():
        zrow[...] = jnp.zeros(zrow.shape, zrow.dtype)
        for e in range(N_EXPERTS):
            start = meta_ref[META_FILL_START, e]
            count = meta_ref[META_FILL_COUNT, e]

            def fill(j, c, start=start):
                pltpu.make_async_copy(zrow, xr_hbm.at[pl.ds(start + j, 1)], zsem).start()
                return c

            def drain(j, c):
                pltpu.make_async_copy(zrow, xr_hbm.at[pl.ds(0, 1)], zsem).wait()
                return c

            lax.fori_loop(0, count, fill, 0)
            lax.fori_loop(0, count, drain, 0)

        zblk[...] = jnp.zeros(zblk.shape, zblk.dtype)
        n_used = meta_ref[META_N_USED, 0]
        n_blocks = xr_hbm.shape[0] // MOE_ROWS

        def tail(b, c):
            cp = pltpu.make_async_copy(zblk, xr_hbm.at[pl.ds(pl.multiple_of(b * MOE_ROWS, MOE_ROWS), MOE_ROWS)], zsem)
            cp.start()
            cp.wait()
            return c

        lax.fori_loop(n_used, n_blocks, tail, 0)

    def send(t, c):
        src = x_ref.at[pl.ds(t, 1)]
        for k in range(TOP_K):
            pltpu.make_async_copy(src, xr_hbm.at[pl.ds(dest_ref[0, TOP_K * t + k], 1)], sem).start()
        return c

    def done(t, c):
        for k in range(TOP_K):
            pltpu.make_async_copy(x_ref.at[pl.ds(0, 1)], xr_hbm.at[pl.ds(0, 1)], sem).wait()
        return c

    lax.fori_loop(0, ts, send, 0, unroll=8)
    lax.fori_loop(0, ts, done, 0, unroll=8)


def _dispatch(xn, dest_rows, meta, n_rows):
    t, d = xn.shape
    nt = dest_rows.shape[0]
    grid_spec = pltpu.PrefetchScalarGridSpec(
        num_scalar_prefetch=1, grid=(nt,),
        in_specs=[pl.BlockSpec((None, 1, dest_rows.shape[2]), lambda i, m: (i, 0, 0),
                               memory_space=pltpu.SMEM),
                  pl.BlockSpec((dest_rows.shape[2] // TOP_K, d), lambda i, m: (i, 0))],
        out_specs=pl.BlockSpec(memory_space=pl.ANY),
        scratch_shapes=[pltpu.VMEM((1, d), xn.dtype), pltpu.VMEM((MOE_ROWS, d), xn.dtype),
                        pltpu.SemaphoreType.DMA(()), pltpu.SemaphoreType.DMA(())])
    return pl.pallas_call(
        _dispatch_kernel, grid_spec=grid_spec,
        out_shape=jax.ShapeDtypeStruct((n_rows, d), xn.dtype),
        compiler_params=_params("arbitrary"), name="moe_dispatch",
    )(meta, dest_rows, xn)


def _expert_kernel(meta_ref, x_ref, wg_hbm, wu_hbm, wd_hbm, y_ref, wg_f32, wu_f32, wd_f32,
                   wg_ref, wu_ref, wd_ref, wsem, seq_ref, *, layer):
    blk = pl.program_id(0)
    n_used = meta_ref[META_N_USED, 0]
    active = blk < n_used
    e_cur = meta_ref[META_BLOCK_E, blk]
    new_expert = (blk == 0) | (e_cur != meta_ref[META_BLOCK_E, jnp.maximum(blk - 1, 0)])
    streams = ((wg_hbm, wg_f32, wg_ref), (wu_hbm, wu_f32, wu_ref), (wd_hbm, wd_f32, wd_ref))

    def weight_copies(e, slot):
        return [pltpu.make_async_copy(src.at[layer, e], stage.at[slot], wsem.at[slot, j])
                for j, (src, stage, _) in enumerate(streams)]

    @pl.when(active & (blk == 0))
    def _():
        seq_ref[0] = 0
        for cp in weight_copies(e_cur, 0):
            cp.start()

    @pl.when(active & new_expert)
    def _():
        slot = seq_ref[0] % 2
        nxt = lax.while_loop(lambda j: (j < n_used) & (meta_ref[META_BLOCK_E, jnp.minimum(j, n_used - 1)] == e_cur),
                             lambda j: j + 1, blk + 1)

        @pl.when(nxt < n_used)
        def _():
            for cp in weight_copies(meta_ref[META_BLOCK_E, nxt], 1 - slot):
                cp.start()

        for cp, (_, stage, dst) in zip(weight_copies(e_cur, slot), streams):
            cp.wait()
            dst[...] = stage[slot].astype(BF16)
        seq_ref[0] = seq_ref[0] + 1

    @pl.when(active)
    def _():
        half = x_ref.shape[1]
        x_lo, x_hi = _unpack_halves(x_ref[...])
        x_lo = x_lo.astype(BF16)
        x_hi = x_hi.astype(BF16)

        def proj(w_ref):
            return (jnp.dot(x_lo, w_ref[0:half, :], preferred_element_type=F32)
                    + jnp.dot(x_hi, w_ref[half:, :], preferred_element_type=F32))

        gate = proj(wg_ref)
        mid = (gate * jax.nn.sigmoid(gate)) * proj(wu_ref)
        y = jnp.dot(mid.astype(BF16), wd_ref[...], preferred_element_type=F32)
        y_ref[...] = _pack_halves(y)

    @pl.when(blk >= meta_ref[META_N_USED, 0])
    def _():
        y_ref[...] = jnp.zeros(y_ref.shape, y_ref.dtype)


def _experts(x_rows, meta, wg, wu, wd, layer):
    n_rows, half = x_rows.shape
    d, ff = wg.shape[-2:]
    assert 2 * half == d
    anyspec = pl.BlockSpec(memory_space=pl.ANY)
    wshapes = [(d, ff), (d, ff), (ff, d)]
    grid_spec = pltpu.PrefetchScalarGridSpec(
        num_scalar_prefetch=1, grid=(n_rows // MOE_ROWS,),
        in_specs=[pl.BlockSpec((MOE_ROWS, half), lambda i, m: (jnp.minimum(i, m[META_N_USED, 0] - 1), 0)),
                  anyspec, anyspec, anyspec],
        out_specs=pl.BlockSpec((MOE_ROWS, half), lambda i, m: (i, 0)),
        scratch_shapes=([pltpu.VMEM((2,) + ws, F32) for ws in wshapes]
                        + [pltpu.VMEM(ws, BF16) for ws in wshapes]
                        + [pltpu.SemaphoreType.DMA((2, len(wshapes))), pltpu.SMEM((1,), jnp.int32)]))
    return pl.pallas_call(
        functools.partial(_expert_kernel, layer=layer), grid_spec=grid_spec,
        out_shape=jax.ShapeDtypeStruct((n_rows, half), jnp.uint32),
        compiler_params=_params("arbitrary"), name="experts",
    )(meta, x_rows, wg, wu, wd)


def _combine_kernel(dest_ref, dnext_ref, h_ref, gate_ref, g_ref, y_hbm, o_ref, ybuf, sem, *, final):
    i = pl.program_id(0)
    nt = pl.num_programs(0)
    ts = h_ref.shape[0]
    slot = i % 2

    def gather(rows_ref, dst_slot):
        def fetch(t, c):
            for k in range(TOP_K):
                pltpu.make_async_copy(y_hbm.at[pl.ds(rows_ref[0, TOP_K * t + k], 1)],
                                      ybuf.at[dst_slot, k, pl.ds(t, 1)], sem.at[dst_slot]).start()
            return c
        lax.fori_loop(0, ts, fetch, 0, unroll=8)

    @pl.when(i == 0)
    def _():
        gather(dest_ref, 0)

    @pl.when(i + 1 < nt)
    def _():
        gather(dnext_ref, 1 - slot)

    def done(t, c):
        for k in range(TOP_K):
            pltpu.make_async_copy(y_hbm.at[pl.ds(0, 1)], ybuf.at[slot, k, pl.ds(0, 1)], sem.at[slot]).wait()
        return c

    lax.fori_loop(0, ts, done, 0, unroll=8)
    gate = gate_ref[...]
    y0_lo, y0_hi = _unpack_halves(ybuf[slot, 0])
    y1_lo, y1_hi = _unpack_halves(ybuf[slot, 1])
    moe = jnp.concatenate([gate[:, 0:1] * y0_lo + gate[:, 1:2] * y1_lo,
                           gate[:, 0:1] * y0_hi + gate[:, 1:2] * y1_hi], axis=1)
    out = h_ref[...] + moe
    o_ref[...] = _rms(out, g_ref[...]) if final else out


def _combine(h2, y_rows, dest_rows, gate, g, final):
    t, d = h2.shape
    nt = dest_rows.shape[0]
    ts = t // nt
    assert TOP_K == 2
    return pl.pallas_call(
        functools.partial(_combine_kernel, final=final), grid=(nt,),
        in_specs=[pl.BlockSpec((None, 1, TOP_K * ts), lambda i: (i, 0, 0), memory_space=pltpu.SMEM),
                  pl.BlockSpec((None, 1, TOP_K * ts), lambda i: (jnp.minimum(i + 1, nt - 1), 0, 0),
                               memory_space=pltpu.SMEM),
                  pl.BlockSpec((ts, d), lambda i: (i, 0)),
                  pl.BlockSpec((ts, LANES), lambda i: (i, 0)),
                  pl.BlockSpec((1, d), lambda i: (0, 0)),
                  pl.BlockSpec(memory_space=pl.ANY)],
        out_specs=pl.BlockSpec((ts, d), lambda i: (i, 0)),
        out_shape=jax.ShapeDtypeStruct(h2.shape, F32),
        scratch_shapes=[pltpu.VMEM((2, TOP_K, ts, d // 2), y_rows.dtype), pltpu.SemaphoreType.DMA((2,))],
        compiler_params=_params("arbitrary"), name="moe_combine",
    )(dest_rows, dest_rows, h2, gate, g.reshape(1, d), y_rows)


def _moe(h, g, rg_w, rg_b, re_w, re_b, wg, wu, wd, layer, final_g, final):
    b, s, d = h.shape
    t = b * s
    h2 = h.reshape(t, d)
    xn, eid, gate = _router(h2, g, rg_w, rg_b, re_w, re_b)
    dest_rows, meta = _plan(eid)
    n_rows = -(-(t * TOP_K + N_EXPERTS * (MOE_ROWS - 1)) // MOE_ROWS) * MOE_ROWS
    assert n_rows // MOE_ROWS <= META_LANES
    x_rows = _dispatch(xn, dest_rows, meta, n_rows)
    y_rows = _experts(x_rows, meta, wg, wu, wd, layer)
    return _combine(h2, y_rows, dest_rows, gate, final_g, final).reshape(b, s, d)


def _kv_kernel(h_ref, gin_ref, wdkvf_ref, glat_ref, wkrf_ref, wukf_ref, wuvf_ref, cs_ref, sn_ref,
               k_ref, v_ref, wdkv_ref, wkr_ref, wuk_ref, wuv_ref):
    _cast_weights_once((pl.program_id(0) == 0) & (pl.program_id(1) == 0),
                       [(wdkvf_ref, wdkv_ref), (wkrf_ref, wkr_ref), (wukf_ref, wuk_ref),
                        (wuvf_ref, wuv_ref)])
    hn = _rms(h_ref[...], gin_ref[...])
    ckv = _rms(_dot(hn, wdkv_ref[...]), glat_ref[...])
    kr = _dot(hn, wkr_ref[...])
    kr = kr * cs_ref[...] + _swap_halves(kr) * sn_ref[...]
    lane = lax.broadcasted_iota(jnp.int32, kr.shape, 1)
    kr = jnp.where(lane < ROPE_DIM, kr, 0.0).astype(BF16)
    kn = _dot(ckv, wuk_ref[...]).astype(BF16)
    vv = _dot(ckv, wuv_ref[...])
    for hd in range(MLA_HEADS):
        k_ref[hd, :, 0:NOPE_DIM] = kn[:, hd * NOPE_DIM:(hd + 1) * NOPE_DIM]
        k_ref[hd, :, NOPE_DIM:QK_PAD] = kr
        v_ref[hd] = vv[:, hd * V_DIM:(hd + 1) * V_DIM].T.astype(BF16)


def _shared_kv(h, cs, sn, gin, wdkv, glat, wkr, wuk, wuv):
    b, s, d = h.shape
    ts = ROW_TILE
    r = wdkv.shape[1]
    wkr_p = jnp.concatenate([wkr, jnp.zeros((d, LANES - ROPE_DIM), F32)], axis=1)
    full = lambda shape: pl.BlockSpec(shape, lambda i, j: (0,) * len(shape))
    tab = pl.BlockSpec((None, ts, LANES), lambda i, j: (i, j, 0))
    wshapes = [(d, r), (d, LANES), (r, MLA_HEADS * NOPE_DIM), (r, MLA_HEADS * V_DIM)]
    return pl.pallas_call(
        _kv_kernel, grid=(b, s // ts),
        in_specs=[pl.BlockSpec((None, ts, d), lambda i, j: (i, j, 0)),
                  full((1, d)), _weight_spec(wshapes[0]), full((1, r)), _weight_spec(wshapes[1]),
                  _weight_spec(wshapes[2]), _weight_spec(wshapes[3]), tab, tab],
        out_specs=[pl.BlockSpec((None, MLA_HEADS, ts, QK_PAD), lambda i, j: (i, 0, j, 0)),
                   pl.BlockSpec((None, MLA_HEADS, V_DIM, ts), lambda i, j: (i, 0, 0, j))],
        out_shape=[jax.ShapeDtypeStruct((b, MLA_HEADS, s, QK_PAD), BF16),
                   jax.ShapeDtypeStruct((b, MLA_HEADS, V_DIM, s), BF16)],
        scratch_shapes=[pltpu.VMEM(ws, BF16) for ws in wshapes],
        compiler_params=_params("arbitrary", "arbitrary"), name="mla_shared_kv",
    )(h, gin.reshape(1, d), wdkv, glat.reshape(1, r), wkr_p, wuk, wuv, cs, sn)


def _q_kernel(h_ref, g_ref, wdqf_ref, glat_ref, wuqf_ref, wqrf_ref, cs_ref, sn_ref, q_ref,
              wdq_ref, wuq_ref, wqr_ref):
    _cast_weights_once((pl.program_id(0) == 0) & (pl.program_id(1) == 0),
                       [(wdqf_ref, wdq_ref), (wuqf_ref, wuq_ref), (wqrf_ref, wqr_ref)])
    xn = _rms(h_ref[...], g_ref[...])
    cq = _rms(_dot(xn, wdq_ref[...]), glat_ref[...])
    scale = (NOPE_DIM + ROPE_DIM) ** -0.5 * LOG2_E
    qn = (_dot(cq, wuq_ref[...]) * scale).astype(BF16)
    qr = _dot(cq, wqr_ref[...]) * scale
    cs = cs_ref[...]
    sn = sn_ref[...]
    lane = lax.broadcasted_iota(jnp.int32, cs.shape, 1)
    for pair in range(MLA_HEADS // 2):
        x = qr[:, pair * LANES:(pair + 1) * LANES]
        rot = x * cs + _swap_halves(x) * sn
        for odd in range(2):
            hd = 2 * pair + odd
            part = pltpu.roll(rot, ROPE_DIM, 1) if odd else rot
            q_ref[hd, :, 0:NOPE_DIM] = qn[:, hd * NOPE_DIM:(hd + 1) * NOPE_DIM]
            q_ref[hd, :, NOPE_DIM:QK_PAD] = jnp.where(lane < ROPE_DIM, part, 0.0).astype(BF16)


def _q_proj(h, cs, sn, g, wdq, glat, wuq, wqr, layer):
    b, s, d = h.shape
    ts = ROW_TILE
    r = wdq.shape[-1]
    full = lambda shape: pl.BlockSpec(shape, lambda i, j: (0,) * len(shape))
    tab = pl.BlockSpec((None, ts, LANES), lambda i, j: (i, j, 0))
    wshapes = [(d, r), (r, MLA_HEADS * NOPE_DIM), (r, MLA_HEADS * ROPE_DIM)]
    return pl.pallas_call(
        _q_kernel, grid=(b, s // ts),
        in_specs=[pl.BlockSpec((None, ts, d), lambda i, j: (i, j, 0)),
                  full((1, d)), _weight_spec(wshapes[0], layer), full((1, r)),
                  _weight_spec(wshapes[1], layer), _weight_spec(wshapes[2], layer), tab, tab],
        out_specs=pl.BlockSpec((None, MLA_HEADS, ts, QK_PAD), lambda i, j: (i, 0, j, 0)),
        out_shape=jax.ShapeDtypeStruct((b, MLA_HEADS, s, QK_PAD), BF16),
        scratch_shapes=[pltpu.VMEM(ws, BF16) for ws in wshapes],
        compiler_params=_params("arbitrary", "arbitrary"), name="mla_q_proj",
    )(h, g.reshape(1, d), wdq, glat.reshape(1, r), wuq, wqr, cs, sn)


def _flash_kernel(q_ref, k_ref, v_ref, o_ref, m_scr, l_scr, acc_scr, s_scr):
    qi = pl.program_id(1)
    ki = pl.program_id(2)
    tq = q_ref.shape[1]
    tk = k_ref.shape[1]

    @pl.when(ki == 0)
    def _():
        m_scr[...] = jnp.full(m_scr.shape, -jnp.inf, F32)
        l_scr[...] = jnp.zeros(l_scr.shape, F32)
        acc_scr[...] = jnp.zeros(acc_scr.shape, F32)

    def scores(hd):
        s_scr[hd % 2] = lax.dot_general(k_ref[hd], q_ref[hd], (((1,), (1,)), ((), ())),
                                        preferred_element_type=F32)

    def softmax_pv(hd, bias):
        s = s_scr[hd % 2]
        if bias is not None:
            s = s + bias
        m_old = m_scr[hd]
        m_new = jnp.maximum(m_old, jnp.max(s, axis=0, keepdims=True))
        alpha = jnp.exp2(m_old - m_new)
        p = jnp.exp2(s - m_new[0:1])
        l_scr[hd] = alpha * l_scr[hd] + jnp.sum(p, axis=0, keepdims=True)
        acc_scr[hd] = alpha[0:1] * acc_scr[hd] + jnp.dot(v_ref[hd], p.astype(BF16),
                                                         preferred_element_type=F32)
        m_scr[hd] = m_new

    def all_heads(bias):
        scores(0)
        for hd in range(MLA_HEADS):
            if hd + 1 < MLA_HEADS:
                scores(hd + 1)
            softmax_pv(hd, bias)

    @pl.when(ki < qi)
    def _():
        all_heads(None)

    @pl.when(ki == qi)
    def _():
        keyc = lax.broadcasted_iota(jnp.int32, (tk, tq), 0) // CHUNK
        qryc = lax.broadcasted_iota(jnp.int32, (tk, tq), 1) // CHUNK
        all_heads(jnp.where(keyc <= qryc, 0.0, -jnp.inf))
        for hd in range(MLA_HEADS):
            o_t = acc_scr[hd] / l_scr[hd][0:1]
            o_ref[:, hd * V_DIM:(hd + 1) * V_DIM] = o_t.T.astype(BF16)


def _flash(q, k, v):
    b, nh, s, _ = q.shape
    tq, tk = ATT_TQ, ATT_TK
    assert tq == tk and tq % CHUNK == 0
    return pl.pallas_call(
        _flash_kernel, grid=(b, s // tq, s // tk),
        in_specs=[pl.BlockSpec((None, nh, tq, QK_PAD), lambda i, qi, ki: (i, 0, qi, 0)),
                  pl.BlockSpec((None, nh, tk, QK_PAD), lambda i, qi, ki: (i, 0, jnp.minimum(ki, qi), 0)),
                  pl.BlockSpec((None, nh, V_DIM, tk), lambda i, qi, ki: (i, 0, 0, jnp.minimum(ki, qi)))],
        out_specs=pl.BlockSpec((None, tq, nh * V_DIM), lambda i, qi, ki: (i, qi, 0)),
        out_shape=jax.ShapeDtypeStruct((b, s, nh * V_DIM), BF16),
        scratch_shapes=[pltpu.VMEM((nh, 8, tq), F32), pltpu.VMEM((nh, 8, tq), F32),
                        pltpu.VMEM((nh, V_DIM, tq), F32), pltpu.VMEM((2, tk, tq), F32)],
        compiler_params=_params("parallel", "parallel", "arbitrary"), name="mla_flash",
    )(q, k, v)


def _oproj_kernel(h_ref, o_ref, wf_ref, out_ref, w_ref):
    _cast_weights_once((pl.program_id(0) == 0) & (pl.program_id(1) == 0), [(wf_ref, w_ref)])
    out_ref[...] = h_ref[...] + jnp.dot(o_ref[...], w_ref[...], preferred_element_type=F32)


def _o_proj(h, o, w_all, layer):
    b, s, d = h.shape
    ts = ROW_TILE
    tile = lambda n: pl.BlockSpec((None, ts, n), lambda i, j: (i, j, 0))
    wshape = w_all.shape[1:]
    return pl.pallas_call(
        _oproj_kernel, grid=(b, s // ts),
        in_specs=[tile(d), tile(o.shape[-1]), _weight_spec(wshape, layer)],
        out_specs=tile(d), out_shape=jax.ShapeDtypeStruct(h.shape, F32),
        scratch_shapes=[pltpu.VMEM(wshape, BF16)],
        compiler_params=_params("arbitrary", "arbitrary"), name="mla_o_proj",
    )(h, o, w_all)


def kernel(x, mem, positions, norm_mix, norm_mem, norm_memtok, norm_ffn, pool_w, pool_scale, kv_in_norm, w_dkv, kv_latent_norm, w_kr, w_uk, w_uv, w_dq, q_latent_norm, w_uq, w_qr, w_o, mem_wq, mem_wk, mem_wv, mem_wo, rg_w, rg_b, re_w, re_b, w_gate, w_up, w_down, final_norm):
    cs, sn = _rope_tables(positions)
    mk, mv = _mem_kv(mem, norm_memtok, mem_wk, mem_wv)
    h = x
    k_sh = v_sh = None
    for l in range(DEPTH):
        if l < N_A_LAYERS:
            h = _pool_layer(h, norm_mix[l], pool_w, pool_scale[l], l)
        else:
            if l == N_A_LAYERS:
                k_sh, v_sh = _shared_kv(h, cs, sn, kv_in_norm, w_dkv, kv_latent_norm, w_kr, w_uk, w_uv)
            j = l - N_A_LAYERS
            q = _q_proj(h, cs, sn, norm_mix[l], w_dq, q_latent_norm[j], w_uq, w_qr, j)
            h = _o_proj(h, _flash(q, k_sh, v_sh), w_o, j)
        h = _mem_attn(h, norm_mem[l], mem_wq, mk, mv, mem_wo, l)
        h = _moe(h, norm_ffn[l], rg_w[l], rg_b[l], re_w[l], re_b[l], w_gate, w_up, w_down, l,
                 final_norm, l == DEPTH - 1)
    return h
```

```python
import functools

import jax
import jax.numpy as jnp
from jax import lax
from jax.experimental import pallas as pl
from jax.experimental.pallas import tpu as pltpu

D_MODEL = 2048
DEPTH = 4
CHUNK = 64
N_A_LAYERS = DEPTH // 2
RMS_EPS = 1e-6
POOL_WINDOWS = (2, 4, 8, 16)
POOL_GROUP = D_MODEL // len(POOL_WINDOWS)
MLA_HEADS = 16
NOPE_DIM = 128
ROPE_DIM = 64
V_DIM = 128
ROPE_THETA = 10000.0
MEM_HEADS = 4
MEM_HEAD_DIM = 128
MEM_INNER = MEM_HEADS * MEM_HEAD_DIM
N_GROUPS = 4
EXPERTS_PER_GROUP = 8
N_EXPERTS = N_GROUPS * EXPERTS_PER_GROUP
TOP_K = 2
EXPERT_FF = D_MODEL // 4

LANES = 128
QK_PAD = 256
HALO = 16
POOL_SUB = 256
MOE_ROWS = 256
ROW_TILE = 512
ATT_TQ = 512
ATT_TK = 512
LOG2_E = 1.4426950408889634
VMEM_LIMIT = 56 * 1024 * 1024

F32 = jnp.float32
BF16 = jnp.bfloat16


def _params(*sem):
    return pltpu.CompilerParams(dimension_semantics=sem, vmem_limit_bytes=VMEM_LIMIT)


def _rms(x, g):
    return x * lax.rsqrt(jnp.mean(x * x, axis=-1, keepdims=True) + RMS_EPS) * g


def _dot(a, b):
    return jnp.dot(a.astype(BF16), b.astype(BF16), preferred_element_type=F32)


def _dot_t(a, b):
    return lax.dot_general(a.astype(BF16), b.astype(BF16), (((1,), (1,)), ((), ())),
                           preferred_element_type=F32)


def _pack_halves(x):
    n = x.shape[1] // 2
    lo = lax.bitcast_convert_type(x[:, :n].astype(BF16).astype(F32), jnp.uint32)
    hi = lax.bitcast_convert_type(x[:, n:].astype(BF16).astype(F32), jnp.uint32)
    return lax.shift_right_logical(lo, jnp.uint32(16)) | (hi & jnp.uint32(0xFFFF0000))


def _unpack_halves(u):
    lo = lax.bitcast_convert_type(lax.shift_left(u, jnp.uint32(16)), F32)
    hi = lax.bitcast_convert_type(u & jnp.uint32(0xFFFF0000), F32)
    return lo, hi


def _weight_spec(shape, layer=None):
    if layer is None:
        return pl.BlockSpec(shape, lambda *_: (0,) * len(shape), pipeline_mode=pl.Buffered(1))
    return pl.BlockSpec((None,) + shape, lambda *_: (layer,) + (0,) * len(shape),
                        pipeline_mode=pl.Buffered(1))


def _cast_weights_once(first, pairs):
    @pl.when(first)
    def _():
        for src, dst in pairs:
            dst[...] = src[...].astype(BF16)


def _swap_halves(x):
    lane = lax.broadcasted_iota(jnp.int32, x.shape, 1)
    return jnp.where((lane & 32) == 0, pltpu.roll(x, 96, 1), pltpu.roll(x, 32, 1))


def _rope_kernel(pos_ref, cs_ref, sn_ref):
    pos = pos_ref[...].astype(F32)
    lane = lax.broadcasted_iota(jnp.int32, (1, LANES), 1)
    half = ROPE_DIM // 2
    frac = (lane % half).astype(F32) / half
    inv = 1.0 / jnp.power(jnp.full((1, LANES), ROPE_THETA, F32), frac)
    ang = pos * inv
    sign = jnp.where((lane & half) == 0, -1.0, 1.0)
    cs_ref[...] = jnp.cos(ang)
    sn_ref[...] = jnp.sin(ang) * sign


def _rope_tables(positions):
    b, s = positions.shape
    ts = ROW_TILE
    out = jax.ShapeDtypeStruct((b, s, LANES), F32)
    spec = pl.BlockSpec((None, ts, LANES), lambda i, j: (i, j, 0))
    return pl.pallas_call(
        _rope_kernel, grid=(b, s // ts),
        in_specs=[pl.BlockSpec((None, ts, 1), lambda i, j: (i, j, 0))],
        out_specs=[spec, spec], out_shape=[out, out],
        compiler_params=_params("parallel", "parallel"), name="rope_tables",
    )(positions.reshape(b, s, 1))


def _pool_kernel(h_ref, halo_ref, g_ref, wf_ref, sc_ref, o_ref, buf_ref, w_ref):
    ts = h_ref.shape[0]
    i = pl.program_id(1)
    _cast_weights_once((pl.program_id(0) == 0) & (i == 0), [(wf_ref, w_ref)])
    g = g_ref[...]
    xn = _rms(h_ref[...], g)
    halo = _rms(halo_ref[...], g)
    buf_ref[0:HALO, :] = jnp.where(i > 0, halo, 0.0)
    buf_ref[HALO:HALO + ts, :] = xn
    r = POOL_SUB
    row = lax.broadcasted_iota(jnp.int32, (r, r + HALO), 0)
    col = lax.broadcasted_iota(jnp.int32, (r, r + HALO), 1)
    off = col - row
    for sub in range(ts // r):
        ext = buf_ref[sub * r:sub * r + r + HALO, :]
        ext_hi = ext.astype(BF16)
        ext_lo = (ext - ext_hi.astype(F32)).astype(BF16)
        t_seq = i * ts + sub * r + lax.broadcasted_iota(jnp.int32, (r, 1), 0)
        for gi, w in enumerate(POOL_WINDOWS):
            cols = slice(gi * POOL_GROUP, (gi + 1) * POOL_GROUP)
            band = jnp.where((off > HALO - w) & (off <= HALO), 1.0, 0.0).astype(BF16)
            wsum = (jnp.dot(band, ext_hi[:, cols], preferred_element_type=F32)
                    + jnp.dot(band, ext_lo[:, cols], preferred_element_type=F32))
            cnt = jnp.minimum(t_seq + 1, w).astype(F32)
            u = wsum / cnt - ext[HALO:, cols]
            y = _dot(u, w_ref[gi])
            rows = slice(sub * r, (sub + 1) * r)
            o_ref[rows, cols] = h_ref[rows, cols] + y * sc_ref[:, cols]


def _pool_layer(h, g, w_all, sc, layer):
    b, s, d = h.shape
    ts = ROW_TILE
    hpt = ts // HALO
    wshape = w_all.shape[1:]
    return pl.pallas_call(
        _pool_kernel, grid=(b, s // ts),
        in_specs=[
            pl.BlockSpec((None, ts, d), lambda i, j: (i, j, 0)),
            pl.BlockSpec((None, HALO, d), lambda i, j: (i, jnp.maximum(j * hpt - 1, 0), 0)),
            pl.BlockSpec((1, d), lambda i, j: (0, 0)),
            _weight_spec(wshape, layer),
            pl.BlockSpec((1, d), lambda i, j: (0, 0)),
        ],
        out_specs=pl.BlockSpec((None, ts, d), lambda i, j: (i, j, 0)),
        out_shape=jax.ShapeDtypeStruct(h.shape, F32),
        scratch_shapes=[pltpu.VMEM((ts + HALO, d), F32), pltpu.VMEM(wshape, BF16)],
        compiler_params=_params("arbitrary", "arbitrary"), name="pool_layer",
    )(h, h, g.reshape(1, d), w_all, sc.reshape(1, d))


def _mem_kv_kernel(m_ref, g_ref, wk_ref, wv_ref, k_ref, v_ref):
    m = _rms(m_ref[...], g_ref[...])
    k_ref[...] = _dot(m, wk_ref[...]).astype(BF16)
    v_ref[...] = _dot(m, wv_ref[...]).astype(BF16)


def _mem_kv(mem, g, wk, wv):
    b, m, d = mem.shape
    nl = g.shape[0]
    out = jax.ShapeDtypeStruct((nl, b, m, MEM_INNER), BF16)
    ospec = pl.BlockSpec((None, None, m, MEM_INNER), lambda l, i: (l, i, 0, 0))
    wspec = pl.BlockSpec((None, d, MEM_INNER), lambda l, i: (l, 0, 0))
    return pl.pallas_call(
        _mem_kv_kernel, grid=(nl, b),
        in_specs=[pl.BlockSpec((None, m, d), lambda l, i: (i, 0, 0)),
                  pl.BlockSpec((None, 1, d), lambda l, i: (l, 0, 0)), wspec, wspec],
        out_specs=[ospec, ospec], out_shape=[out, out],
        compiler_params=_params("arbitrary", "arbitrary"), name="mem_kv",
    )(mem, g.reshape(nl, 1, d), wk, wv)


def _mem_attn_kernel(h_ref, g_ref, wqf_ref, k_ref, v_ref, wof_ref, rg_ref, rw2_ref, rwh_ref, rb_ref,
                     o_ref, xn_ref, eid_ref, gate_ref, wq_ref, wo_ref):
    _cast_weights_once((pl.program_id(0) == 0) & (pl.program_id(1) == 0),
                       [(wqf_ref, wq_ref), (wof_ref, wo_ref)])
    h = h_ref[...]
    q = _dot(_rms(h, g_ref[...]), wq_ref[...]).astype(BF16)
    scale = MEM_HEAD_DIM ** -0.5
    heads = []
    for hd in range(MEM_HEADS):
        cols = slice(hd * MEM_HEAD_DIM, (hd + 1) * MEM_HEAD_DIM)
        sc = _dot_t(q[:, cols], k_ref[:, cols]) * scale
        p = jnp.exp(sc - jnp.max(sc, axis=-1, keepdims=True))
        den = jnp.sum(p, axis=-1, keepdims=True)
        heads.append(_dot(p, v_ref[:, cols]) / den)
    o = jnp.concatenate(heads, axis=-1)
    h_new = h + _dot(o, wo_ref[...])
    o_ref[...] = h_new
    _route(h_new, rg_ref, rw2_ref, rwh_ref, rb_ref, xn_ref, eid_ref, gate_ref)


def _mem_attn_route(h, g, wq, k, v, wo, layer, router_ops):
    b, s, d = h.shape
    ts = ROW_TILE
    nts = s // ts
    t = b * s
    m = k.shape[2]
    kvspec = pl.BlockSpec((None, None, m, MEM_INNER), lambda i, j: (layer, i, 0, 0))
    full = lambda a: pl.BlockSpec(a.shape, lambda i, j: (0,) * a.ndim)
    tok = lambda n: pl.BlockSpec((ts, n), lambda i, j: (i * nts + j, 0))
    row = lambda n, dt: jax.ShapeDtypeStruct((t, n), dt)
    return pl.pallas_call(
        _mem_attn_kernel, grid=(b, nts),
        in_specs=[pl.BlockSpec((None, ts, d), lambda i, j: (i, j, 0)),
                  pl.BlockSpec((1, d), lambda i, j: (0, 0)),
                  _weight_spec((d, MEM_INNER), layer),
                  kvspec, kvspec,
                  _weight_spec((MEM_INNER, d), layer)] + [full(a) for a in router_ops],
        out_specs=[pl.BlockSpec((None, ts, d), lambda i, j: (i, j, 0)),
                   tok(d // 2), tok(LANES), tok(LANES)],
        out_shape=[jax.ShapeDtypeStruct(h.shape, F32),
                   row(d // 2, jnp.uint32), row(LANES, jnp.int32), row(LANES, F32)],
        scratch_shapes=[pltpu.VMEM((d, MEM_INNER), BF16), pltpu.VMEM((MEM_INNER, d), BF16)],
        compiler_params=_params("arbitrary", "arbitrary"), name="mem_attn_route",
    )(h, g.reshape(1, d), wq, k, v, wo, *router_ops)


def _route(h, g_ref, w2_ref, wh_ref, b_ref, xn_ref, eid_ref, gate_ref):
    xn = _rms(h, g_ref[...])
    xn_ref[...] = _pack_halves(xn)
    x_hi = xn.astype(BF16)
    x_lo = (xn - x_hi.astype(F32)).astype(BF16)
    a = jnp.dot(x_hi, w2_ref[...], preferred_element_type=F32)
    c = jnp.dot(x_lo, wh_ref[...], preferred_element_type=F32)
    logits = a[:, :LANES] + a[:, LANES:] + c + b_ref[...]
    lane = lax.broadcasted_iota(jnp.int32, logits.shape, 1)
    neg = -jnp.inf
    gl = jnp.where(lane < N_GROUPS, logits, neg)
    gmax = jnp.max(gl, axis=-1, keepdims=True)
    g_sel = jnp.min(jnp.where(gl == gmax, lane, LANES), axis=-1, keepdims=True)
    g_p = 1.0 / jnp.sum(jnp.exp(gl - gmax), axis=-1, keepdims=True)
    e_idx = lane - N_GROUPS
    in_grp = (e_idx >= 0) & (e_idx < N_EXPERTS) & ((e_idx >> 3) == g_sel)
    assert EXPERTS_PER_GROUP == 8
    el = jnp.where(in_grp, logits, neg)
    m1 = jnp.max(el, axis=-1, keepdims=True)
    i1 = jnp.min(jnp.where(el == m1, lane, LANES), axis=-1, keepdims=True)
    el2 = jnp.where(lane == i1, neg, el)
    m2 = jnp.max(el2, axis=-1, keepdims=True)
    i2 = jnp.min(jnp.where(el2 == m2, lane, LANES), axis=-1, keepdims=True)
    esum = jnp.sum(jnp.exp(el - m1), axis=-1, keepdims=True)
    p1 = 1.0 / esum
    p2 = jnp.exp(m2 - m1) / esum
    w1 = g_p * p1 / (p1 + p2)
    w2 = g_p * p2 / (p1 + p2)
    eid_ref[...] = jnp.where(lane == 0, i1 - N_GROUPS, jnp.where(lane == 1, i2 - N_GROUPS, 0))
    gate_ref[...] = jnp.where(lane == 0, w1, jnp.where(lane == 1, w2, 0.0))


def _router_operands(g, rg_w, rg_b, re_w, re_b):
    d = rg_w.shape[0]
    pad = LANES - N_GROUPS - N_EXPERTS
    w = jnp.concatenate([rg_w, re_w, jnp.zeros((d, pad), F32)], axis=1)
    bias = jnp.concatenate([rg_b, re_b, jnp.zeros((pad,), F32)]).reshape(1, LANES)
    w_hi = w.astype(BF16)
    w_lo = (w - w_hi.astype(F32)).astype(BF16)
    return g.reshape(1, d), jnp.concatenate([w_hi, w_lo], axis=1), w_hi, bias


META_BLOCK_E, META_N_USED, META_FILL_START, META_FILL_COUNT = 0, 1, 2, 3
META_LANES = 2 * LANES


def _onehot(eid, k):
    lane = lax.broadcasted_iota(jnp.int32, eid.shape, 1)
    return lane == eid[:, k:k + 1]


def _plan_kernel(eid_ref, dest_ref, meta_ref, cnt_scr, carry_scr):
    ph = pl.program_id(0)
    i = pl.program_id(1)
    ts = eid_ref.shape[0]
    eid = eid_ref[...]
    o0 = _onehot(eid, 0)
    o1 = _onehot(eid, 1)
    osum = jnp.where(o0, 1.0, 0.0) + jnp.where(o1, 1.0, 0.0)
    tile_cnt = jnp.sum(osum, axis=0, keepdims=True)

    @pl.when((ph == 0) & (i == 0))
    def _():
        cnt_scr[...] = jnp.zeros(cnt_scr.shape, F32)

    @pl.when(ph == 0)
    def _():
        cnt_scr[...] = cnt_scr[...] + tile_cnt

    @pl.when(ph == 1)
    def _():
        cnt = cnt_scr[...]
        nblk = jnp.floor((cnt + (MOE_ROWS - 1)) * (1.0 / MOE_ROWS))
        er = lax.broadcasted_iota(jnp.int32, (LANES, LANES), 0)
        ec = lax.broadcasted_iota(jnp.int32, (LANES, LANES), 1)
        before = jnp.where(er < ec, 1.0, 0.0).astype(BF16)
        bstart = jnp.dot(nblk.astype(BF16), before, preferred_element_type=F32)
        row_start = bstart * MOE_ROWS

        @pl.when(i == 0)
        def _():
            carry_scr[...] = jnp.zeros(carry_scr.shape, F32)
            bend = bstart + nblk
            bend_col = jnp.sum(jnp.where(er == ec, bend[0:1], 0.0), axis=1, keepdims=True)
            e2 = lax.broadcasted_iota(jnp.int32, (LANES, META_LANES), 0)
            b2 = lax.broadcasted_iota(jnp.int32, (LANES, META_LANES), 1).astype(F32)
            done = jnp.where((bend_col <= b2) & (e2 < N_EXPERTS), 1.0, 0.0)
            block_e = jnp.minimum(jnp.sum(done, axis=0, keepdims=True), N_EXPERTS - 1.0)
            n_used = jnp.sum(nblk[0:1], axis=1, keepdims=True)
            zpad = jnp.zeros((1, META_LANES - LANES), F32)
            fill_start = jnp.concatenate([row_start[0:1] + cnt[0:1], zpad], axis=1)
            fill_count = jnp.concatenate([nblk[0:1] * MOE_ROWS - cnt[0:1], zpad], axis=1)
            row = lax.broadcasted_iota(jnp.int32, meta_ref.shape, 0)
            meta = jnp.where(row == META_BLOCK_E, block_e,
                             jnp.where(row == META_N_USED, n_used,
                                       jnp.where(row == META_FILL_START, fill_start,
                                                 jnp.where(row == META_FILL_COUNT, fill_count, 0.0))))
            meta_ref[...] = meta.astype(jnp.int32)

        rr = lax.broadcasted_iota(jnp.int32, (ts, ts), 0)
        rc = lax.broadcasted_iota(jnp.int32, (ts, ts), 1)
        earlier = jnp.where(rc < rr, 1.0, 0.0).astype(BF16)
        base = (jnp.dot(earlier, osum.astype(BF16), preferred_element_type=F32)
                + carry_scr[0:1] + row_start[0:1])
        d0 = jnp.sum(jnp.where(o0, base, 0.0), axis=1, keepdims=True)
        d1 = jnp.sum(jnp.where(o1, base + jnp.where(o0, 1.0, 0.0), 0.0), axis=1, keepdims=True)
        lane = lax.broadcasted_iota(jnp.int32, (ts, LANES), 1)
        dest_ref[...] = jnp.where(lane == 0, d0, jnp.where(lane == 1, d1, 0.0)).astype(jnp.int32)
        carry_scr[...] = carry_scr[...] + tile_cnt


def _plan(eid):
    t = eid.shape[0]
    ts = ROW_TILE
    dest, meta = pl.pallas_call(
        _plan_kernel, grid=(2, t // ts),
        in_specs=[pl.BlockSpec((ts, LANES), lambda ph, i: (i, 0))],
        out_specs=[pl.BlockSpec((ts, LANES), lambda ph, i: (ph * i, 0)),
                   pl.BlockSpec((8, META_LANES), lambda ph, i: (0, 0))],
        out_shape=[jax.ShapeDtypeStruct((t, LANES), jnp.int32),
                   jax.ShapeDtypeStruct((8, META_LANES), jnp.int32)],
        scratch_shapes=[pltpu.VMEM((8, LANES), F32), pltpu.VMEM((8, LANES), F32)],
        compiler_params=_params("arbitrary", "arbitrary"), name="moe_plan",
    )(eid)
    dest_rows = dest[:, :TOP_K].reshape(t // ts, 1, TOP_K * ts)
    return dest_rows, meta


def _dispatch_kernel(meta_ref, dest_ref, x_ref, xr_hbm, zrow, zblk, zsem, sem):
    i = pl.program_id(0)
    ts = dest_ref.shape[1] // TOP_K

    @pl.when(i == 0)
    def _():
        zrow[...] = jnp.zeros(zrow.shape, zrow.dtype)
        for e in range(N_EXPERTS):
            start = meta_ref[META_FILL_START, e]
            count = meta_ref[META_FILL_COUNT, e]

            def fill(j, c, start=start):
                pltpu.make_async_copy(zrow, xr_hbm.at[pl.ds(start + j, 1)], zsem).start()
                return c

            def drain(j, c):
                pltpu.make_async_copy(zrow, xr_hbm.at[pl.ds(0, 1)], zsem).wait()
                return c

            lax.fori_loop(0, count, fill, 0)
            lax.fori_loop(0, count, drain, 0)

        zblk[...] = jnp.zeros(zblk.shape, zblk.dtype)
        n_used = meta_ref[META_N_USED, 0]
        n_blocks = xr_hbm.shape[0] // MOE_ROWS

        def tail(b, c):
            cp = pltpu.make_async_copy(zblk, xr_hbm.at[pl.ds(pl.multiple_of(b * MOE_ROWS, MOE_ROWS), MOE_ROWS)], zsem)
            cp.start()
            cp.wait()
            return c

        lax.fori_loop(n_used, n_blocks, tail, 0)

    for t in range(ts):
        for k in range(TOP_K):
            pltpu.make_async_copy(x_ref.at[pl.ds(t, 1)],
                                  xr_hbm.at[pl.ds(dest_ref[0, TOP_K * t + k], 1)], sem).start()
    for _ in range(ts * TOP_K):
        pltpu.make_async_copy(x_ref.at[pl.ds(0, 1)], xr_hbm.at[pl.ds(0, 1)], sem).wait()


def _dispatch(xn, dest_rows, meta, n_rows):
    t, d = xn.shape
    nt = dest_rows.shape[0]
    grid_spec = pltpu.PrefetchScalarGridSpec(
        num_scalar_prefetch=1, grid=(nt,),
        in_specs=[pl.BlockSpec((None, 1, dest_rows.shape[2]), lambda i, m: (i, 0, 0),
                               memory_space=pltpu.SMEM),
                  pl.BlockSpec((dest_rows.shape[2] // TOP_K, d), lambda i, m: (i, 0))],
        out_specs=pl.BlockSpec(memory_space=pl.ANY),
        scratch_shapes=[pltpu.VMEM((1, d), xn.dtype), pltpu.VMEM((MOE_ROWS, d), xn.dtype),
                        pltpu.SemaphoreType.DMA(()), pltpu.SemaphoreType.DMA(())])
    return pl.pallas_call(
        _dispatch_kernel, grid_spec=grid_spec,
        out_shape=jax.ShapeDtypeStruct((n_rows, d), xn.dtype),
        compiler_params=_params("arbitrary"), name="moe_dispatch",
    )(meta, dest_rows, xn)


def _expert_kernel(meta_ref, x_ref, wg_hbm, wu_hbm, wd_hbm, y_ref, wg_f32, wu_f32, wd_f32,
                   wg_ref, wu_ref, wd_ref, wsem, seq_ref, *, layer):
    blk = pl.program_id(0)
    n_used = meta_ref[META_N_USED, 0]
    active = blk < n_used
    e_cur = meta_ref[META_BLOCK_E, blk]
    new_expert = (blk == 0) | (e_cur != meta_ref[META_BLOCK_E, jnp.maximum(blk - 1, 0)])
    streams = ((wg_hbm, wg_f32, wg_ref), (wu_hbm, wu_f32, wu_ref), (wd_hbm, wd_f32, wd_ref))

    def weight_copies(e, slot):
        return [pltpu.make_async_copy(src.at[layer, e], stage.at[slot], wsem.at[slot, j])
                for j, (src, stage, _) in enumerate(streams)]

    @pl.when(active & (blk == 0))
    def _():
        seq_ref[0] = 0
        for cp in weight_copies(e_cur, 0):
            cp.start()

    @pl.when(active & new_expert)
    def _():
        slot = seq_ref[0] % 2
        nxt = lax.while_loop(lambda j: (j < n_used) & (meta_ref[META_BLOCK_E, jnp.minimum(j, n_used - 1)] == e_cur),
                             lambda j: j + 1, blk + 1)

        @pl.when(nxt < n_used)
        def _():
            for cp in weight_copies(meta_ref[META_BLOCK_E, nxt], 1 - slot):
                cp.start()

        for cp, (_, stage, dst) in zip(weight_copies(e_cur, slot), streams):
            cp.wait()
            dst[...] = stage[slot].astype(BF16)
        seq_ref[0] = seq_ref[0] + 1

    @pl.when(active)
    def _():
        half = x_ref.shape[1]
        x_lo, x_hi = _unpack_halves(x_ref[...])
        x_lo = x_lo.astype(BF16)
        x_hi = x_hi.astype(BF16)

        def proj(w_ref):
            return (jnp.dot(x_lo, w_ref[0:half, :], preferred_element_type=F32)
                    + jnp.dot(x_hi, w_ref[half:, :], preferred_element_type=F32))

        gate = proj(wg_ref)
        mid = (gate * jax.nn.sigmoid(gate)) * proj(wu_ref)
        y = jnp.dot(mid.astype(BF16), wd_ref[...], preferred_element_type=F32)
        y_ref[...] = _pack_halves(y)

    @pl.when(blk >= meta_ref[META_N_USED, 0])
    def _():
        y_ref[...] = jnp.zeros(y_ref.shape, y_ref.dtype)


def _experts(x_rows, meta, wg, wu, wd, layer):
    n_rows, half = x_rows.shape
    d, ff = wg.shape[-2:]
    assert 2 * half == d
    anyspec = pl.BlockSpec(memory_space=pl.ANY)
    wshapes = [(d, ff), (d, ff), (ff, d)]
    grid_spec = pltpu.PrefetchScalarGridSpec(
        num_scalar_prefetch=1, grid=(n_rows // MOE_ROWS,),
        in_specs=[pl.BlockSpec((MOE_ROWS, half), lambda i, m: (jnp.minimum(i, m[META_N_USED, 0] - 1), 0)),
                  anyspec, anyspec, anyspec],
        out_specs=pl.BlockSpec((MOE_ROWS, half), lambda i, m: (i, 0)),
        scratch_shapes=([pltpu.VMEM((2,) + ws, F32) for ws in wshapes]
                        + [pltpu.VMEM(ws, BF16) for ws in wshapes]
                        + [pltpu.SemaphoreType.DMA((2, len(wshapes))), pltpu.SMEM((1,), jnp.int32)]))
    return pl.pallas_call(
        functools.partial(_expert_kernel, layer=layer), grid_spec=grid_spec,
        out_shape=jax.ShapeDtypeStruct((n_rows, half), jnp.uint32),
        compiler_params=_params("arbitrary"), name="experts",
    )(meta, x_rows, wg, wu, wd)


def _combine_kernel(dest_ref, dnext_ref, h_ref, gate_ref, g_ref, y_hbm, o_ref, ybuf, sem, *, final):
    i = pl.program_id(0)
    nt = pl.num_programs(0)
    ts = h_ref.shape[0]
    slot = i % 2

    def gather(rows_ref, dst_slot):
        def fetch(t, c):
            for k in range(TOP_K):
                pltpu.make_async_copy(y_hbm.at[pl.ds(rows_ref[0, TOP_K * t + k], 1)],
                                      ybuf.at[dst_slot, k, pl.ds(t, 1)], sem.at[dst_slot]).start()
            return c
        lax.fori_loop(0, ts, fetch, 0, unroll=8)

    @pl.when(i == 0)
    def _():
        gather(dest_ref, 0)

    for par in range(2):
        @pl.when((i + 1 < nt) & (slot == par))
        def _(par=par):
            for t in range(ts):
                for k in range(TOP_K):
                    pltpu.make_async_copy(y_hbm.at[pl.ds(dnext_ref[0, TOP_K * t + k], 1)],
                                          ybuf.at[1 - par, k, pl.ds(t, 1)], sem.at[1 - par]).start()

    def done(t, c):
        for k in range(TOP_K):
            pltpu.make_async_copy(y_hbm.at[pl.ds(0, 1)], ybuf.at[slot, k, pl.ds(0, 1)], sem.at[slot]).wait()
        return c

    lax.fori_loop(0, ts, done, 0, unroll=8)
    gate = gate_ref[...]
    y0_lo, y0_hi = _unpack_halves(ybuf[slot, 0])
    y1_lo, y1_hi = _unpack_halves(ybuf[slot, 1])
    moe = jnp.concatenate([gate[:, 0:1] * y0_lo + gate[:, 1:2] * y1_lo,
                           gate[:, 0:1] * y0_hi + gate[:, 1:2] * y1_hi], axis=1)
    out = h_ref[...] + moe
    o_ref[...] = _rms(out, g_ref[...]) if final else out


def _combine(h2, y_rows, dest_rows, gate, g, final):
    t, d = h2.shape
    nt = dest_rows.shape[0]
    ts = t // nt
    assert TOP_K == 2
    return pl.pallas_call(
        functools.partial(_combine_kernel, final=final), grid=(nt,),
        in_specs=[pl.BlockSpec((None, 1, TOP_K * ts), lambda i: (i, 0, 0), memory_space=pltpu.SMEM),
                  pl.BlockSpec((None, 1, TOP_K * ts), lambda i: (jnp.minimum(i + 1, nt - 1), 0, 0),
                               memory_space=pltpu.SMEM),
                  pl.BlockSpec((ts, d), lambda i: (i, 0)),
                  pl.BlockSpec((ts, LANES), lambda i: (i, 0)),
                  pl.BlockSpec((1, d), lambda i: (0, 0)),
                  pl.BlockSpec(memory_space=pl.ANY)],
        out_specs=pl.BlockSpec((ts, d), lambda i: (i, 0)),
        out_shape=jax.ShapeDtypeStruct(h2.shape, F32),
        scratch_shapes=[pltpu.VMEM((2, TOP_K, ts, d // 2), y_rows.dtype), pltpu.SemaphoreType.DMA((2,))],
        compiler_params=_params("arbitrary"), name="moe_combine",
    )(dest_rows, dest_rows, h2, gate, g.reshape(1, d), y_rows)


def _moe(h, xn, eid, gate, wg, wu, wd, layer, final_g, final):
    b, s, d = h.shape
    t = b * s
    h2 = h.reshape(t, d)
    dest_rows, meta = _plan(eid)
    n_rows = -(-(t * TOP_K + N_EXPERTS * (MOE_ROWS - 1)) // MOE_ROWS) * MOE_ROWS
    assert n_rows // MOE_ROWS <= META_LANES
    x_rows = _dispatch(xn, dest_rows, meta, n_rows)
    y_rows = _experts(x_rows, meta, wg, wu, wd, layer)
    return _combine(h2, y_rows, dest_rows, gate, final_g, final).reshape(b, s, d)


def _kv_kernel(h_ref, gin_ref, wdkvf_ref, glat_ref, wkrf_ref, wukf_ref, wuvf_ref, cs_ref, sn_ref,
               k_ref, v_ref, wdkv_ref, wkr_ref, wuk_ref, wuv_ref):
    _cast_weights_once((pl.program_id(0) == 0) & (pl.program_id(1) == 0),
                       [(wdkvf_ref, wdkv_ref), (wkrf_ref, wkr_ref), (wukf_ref, wuk_ref),
                        (wuvf_ref, wuv_ref)])
    hn = _rms(h_ref[...], gin_ref[...])
    ckv = _rms(_dot(hn, wdkv_ref[...]), glat_ref[...])
    kr = _dot(hn, wkr_ref[...])
    kr = kr * cs_ref[...] + _swap_halves(kr) * sn_ref[...]
    lane = lax.broadcasted_iota(jnp.int32, kr.shape, 1)
    kr = jnp.where(lane < ROPE_DIM, kr, 0.0).astype(BF16)
    kn = _dot(ckv, wuk_ref[...]).astype(BF16)
    vv = _dot(ckv, wuv_ref[...])
    for hd in range(MLA_HEADS):
        k_ref[hd, :, 0:NOPE_DIM] = kn[:, hd * NOPE_DIM:(hd + 1) * NOPE_DIM]
        k_ref[hd, :, NOPE_DIM:QK_PAD] = kr
        v_ref[hd] = vv[:, hd * V_DIM:(hd + 1) * V_DIM].T.astype(BF16)


def _shared_kv(h, cs, sn, gin, wdkv, glat, wkr, wuk, wuv):
    b, s, d = h.shape
    ts = ROW_TILE
    r = wdkv.shape[1]
    wkr_p = jnp.concatenate([wkr, jnp.zeros((d, LANES - ROPE_DIM), F32)], axis=1)
    full = lambda shape: pl.BlockSpec(shape, lambda i, j: (0,) * len(shape))
    tab = pl.BlockSpec((None, ts, LANES), lambda i, j: (i, j, 0))
    wshapes = [(d, r), (d, LANES), (r, MLA_HEADS * NOPE_DIM), (r, MLA_HEADS * V_DIM)]
    return pl.pallas_call(
        _kv_kernel, grid=(b, s // ts),
        in_specs=[pl.BlockSpec((None, ts, d), lambda i, j: (i, j, 0)),
                  full((1, d)), _weight_spec(wshapes[0]), full((1, r)), _weight_spec(wshapes[1]),
                  _weight_spec(wshapes[2]), _weight_spec(wshapes[3]), tab, tab],
        out_specs=[pl.BlockSpec((None, MLA_HEADS, ts, QK_PAD), lambda i, j: (i, 0, j, 0)),
                   pl.BlockSpec((None, MLA_HEADS, V_DIM, ts), lambda i, j: (i, 0, 0, j))],
        out_shape=[jax.ShapeDtypeStruct((b, MLA_HEADS, s, QK_PAD), BF16),
                   jax.ShapeDtypeStruct((b, MLA_HEADS, V_DIM, s), BF16)],
        scratch_shapes=[pltpu.VMEM(ws, BF16) for ws in wshapes],
        compiler_params=_params("arbitrary", "arbitrary"), name="mla_shared_kv",
    )(h, gin.reshape(1, d), wdkv, glat.reshape(1, r), wkr_p, wuk, wuv, cs, sn)


def _q_kernel(h_ref, g_ref, wdqf_ref, glat_ref, wuqf_ref, wqrf_ref, cs_ref, sn_ref, q_ref,
              wdq_ref, wuq_ref, wqr_ref):
    _cast_weights_once((pl.program_id(0) == 0) & (pl.program_id(1) == 0),
                       [(wdqf_ref, wdq_ref), (wuqf_ref, wuq_ref), (wqrf_ref, wqr_ref)])
    xn = _rms(h_ref[...], g_ref[...])
    cq = _rms(_dot(xn, wdq_ref[...]), glat_ref[...])
    scale = (NOPE_DIM + ROPE_DIM) ** -0.5 * LOG2_E
    qn = (_dot(cq, wuq_ref[...]) * scale).astype(BF16)
    qr = _dot(cq, wqr_ref[...]) * scale
    cs = cs_ref[...]
    sn = sn_ref[...]
    lane = lax.broadcasted_iota(jnp.int32, cs.shape, 1)
    for pair in range(MLA_HEADS // 2):
        x = qr[:, pair * LANES:(pair + 1) * LANES]
        rot = x * cs + _swap_halves(x) * sn
        for odd in range(2):
            hd = 2 * pair + odd
            part = pltpu.roll(rot, ROPE_DIM, 1) if odd else rot
            q_ref[hd, :, 0:NOPE_DIM] = qn[:, hd * NOPE_DIM:(hd + 1) * NOPE_DIM]
            q_ref[hd, :, NOPE_DIM:QK_PAD] = jnp.where(lane < ROPE_DIM, part, 0.0).astype(BF16)


def _q_proj(h, cs, sn, g, wdq, glat, wuq, wqr, layer):
    b, s, d = h.shape
    ts = ROW_TILE
    r = wdq.shape[-1]
    full = lambda shape: pl.BlockSpec(shape, lambda i, j: (0,) * len(shape))
    tab = pl.BlockSpec((None, ts, LANES), lambda i, j: (i, j, 0))
    wshapes = [(d, r), (r, MLA_HEADS * NOPE_DIM), (r, MLA_HEADS * ROPE_DIM)]
    return pl.pallas_call(
        _q_kernel, grid=(b, s // ts),
        in_specs=[pl.BlockSpec((None, ts, d), lambda i, j: (i, j, 0)),
                  full((1, d)), _weight_spec(wshapes[0], layer), full((1, r)),
                  _weight_spec(wshapes[1], layer), _weight_spec(wshapes[2], layer), tab, tab],
        out_specs=pl.BlockSpec((None, MLA_HEADS, ts, QK_PAD), lambda i, j: (i, 0, j, 0)),
        out_shape=jax.ShapeDtypeStruct((b, MLA_HEADS, s, QK_PAD), BF16),
        scratch_shapes=[pltpu.VMEM(ws, BF16) for ws in wshapes],
        compiler_params=_params("arbitrary", "arbitrary"), name="mla_q_proj",
    )(h, g.reshape(1, d), wdq, glat.reshape(1, r), wuq, wqr, cs, sn)


def _flash_kernel(q_ref, k_ref, v_ref, o_ref, m_scr, l_scr, acc_scr, s_scr):
    qi = pl.program_id(1)
    ki = pl.program_id(2)
    tq = q_ref.shape[1]
    tk = k_ref.shape[1]

    @pl.when(ki == 0)
    def _():
        m_scr[...] = jnp.full(m_scr.shape, -jnp.inf, F32)
        l_scr[...] = jnp.zeros(l_scr.shape, F32)
        acc_scr[...] = jnp.zeros(acc_scr.shape, F32)

    def scores(hd):
        s_scr[hd % 2] = lax.dot_general(k_ref[hd], q_ref[hd], (((1,), (1,)), ((), ())),
                                        preferred_element_type=F32)

    def softmax_pv(hd, bias):
        s = s_scr[hd % 2]
        if bias is not None:
            s = s + bias
        m_old = m_scr[hd]
        m_new = jnp.maximum(m_old, jnp.max(s, axis=0, keepdims=True))
        alpha = jnp.exp2(m_old - m_new)
        p = jnp.exp2(s - m_new[0:1])
        l_scr[hd] = alpha * l_scr[hd] + jnp.sum(p, axis=0, keepdims=True)
        acc_scr[hd] = alpha[0:1] * acc_scr[hd] + jnp.dot(v_ref[hd], p.astype(BF16),
                                                         preferred_element_type=F32)
        m_scr[hd] = m_new

    def all_heads(bias):
        scores(0)
        for hd in range(MLA_HEADS):
            if hd + 1 < MLA_HEADS:
                scores(hd + 1)
            softmax_pv(hd, bias)

    @pl.when(ki < qi)
    def _():
        all_heads(None)

    @pl.when(ki == qi)
    def _():
        keyc = lax.broadcasted_iota(jnp.int32, (tk, tq), 0) // CHUNK
        qryc = lax.broadcasted_iota(jnp.int32, (tk, tq), 1) // CHUNK
        all_heads(jnp.where(keyc <= qryc, 0.0, -jnp.inf))
        for hd in range(MLA_HEADS):
            o_t = acc_scr[hd] / l_scr[hd][0:1]
            o_ref[:, hd * V_DIM:(hd + 1) * V_DIM] = o_t.T.astype(BF16)


def _flash(q, k, v):
    b, nh, s, _ = q.shape
    tq, tk = ATT_TQ, ATT_TK
    assert tq == tk and tq % CHUNK == 0
    return pl.pallas_call(
        _flash_kernel, grid=(b, s // tq, s // tk),
        in_specs=[pl.BlockSpec((None, nh, tq, QK_PAD), lambda i, qi, ki: (i, 0, qi, 0)),
                  pl.BlockSpec((None, nh, tk, QK_PAD), lambda i, qi, ki: (i, 0, jnp.minimum(ki, qi), 0)),
                  pl.BlockSpec((None, nh, V_DIM, tk), lambda i, qi, ki: (i, 0, 0, jnp.minimum(ki, qi)))],
        out_specs=pl.BlockSpec((None, tq, nh * V_DIM), lambda i, qi, ki: (i, qi, 0)),
        out_shape=jax.ShapeDtypeStruct((b, s, nh * V_DIM), BF16),
        scratch_shapes=[pltpu.VMEM((nh, 8, tq), F32), pltpu.VMEM((nh, 8, tq), F32),
                        pltpu.VMEM((nh, V_DIM, tq), F32), pltpu.VMEM((2, tk, tq), F32)],
        compiler_params=_params("parallel", "parallel", "arbitrary"), name="mla_flash",
    )(q, k, v)


def _oproj_kernel(h_ref, o_ref, wf_ref, out_ref, w_ref):
    _cast_weights_once((pl.program_id(0) == 0) & (pl.program_id(1) == 0), [(wf_ref, w_ref)])
    out_ref[...] = h_ref[...] + jnp.dot(o_ref[...], w_ref[...], preferred_element_type=F32)


def _o_proj(h, o, w_all, layer):
    b, s, d = h.shape
    ts = ROW_TILE
    tile = lambda n: pl.BlockSpec((None, ts, n), lambda i, j: (i, j, 0))
    wshape = w_all.shape[1:]
    return pl.pallas_call(
        _oproj_kernel, grid=(b, s // ts),
        in_specs=[tile(d), tile(o.shape[-1]), _weight_spec(wshape, layer)],
        out_specs=tile(d), out_shape=jax.ShapeDtypeStruct(h.shape, F32),
        scratch_shapes=[pltpu.VMEM(wshape, BF16)],
        compiler_params=_params("arbitrary", "arbitrary"), name="mla_o_proj",
    )(h, o, w_all)


def kernel(x, mem, positions, norm_mix, norm_mem, norm_memtok, norm_ffn, pool_w, pool_scale, kv_in_norm, w_dkv, kv_latent_norm, w_kr, w_uk, w_uv, w_dq, q_latent_norm, w_uq, w_qr, w_o, mem_wq, mem_wk, mem_wv, mem_wo, rg_w, rg_b, re_w, re_b, w_gate, w_up, w_down, final_norm):
    cs, sn = _rope_tables(positions)
    mk, mv = _mem_kv(mem, norm_memtok, mem_wk, mem_wv)
    h = x
    k_sh = v_sh = None
    for l in range(DEPTH):
        if l < N_A_LAYERS:
            h = _pool_layer(h, norm_mix[l], pool_w, pool_scale[l], l)
        else:
            if l == N_A_LAYERS:
                k_sh, v_sh = _shared_kv(h, cs, sn, kv_in_norm, w_dkv, kv_latent_norm, w_kr, w_uk, w_uv)
            j = l - N_A_LAYERS
            q = _q_proj(h, cs, sn, norm_mix[l], w_dq, q_latent_norm[j], w_uq, w_qr, j)
            h = _o_proj(h, _flash(q, k_sh, v_sh), w_o, j)
        router_ops = _router_operands(norm_ffn[l], rg_w[l], rg_b[l], re_w[l], re_b[l])
        h, xn, eid, gate = _mem_attn_route(h, norm_mem[l], mem_wq, mk, mv, mem_wo, l, router_ops)
        h = _moe(h, xn, eid, gate, w_gate, w_up, w_down, l, final_norm, l == DEPTH - 1)
    return h
```

```python
import functools

import jax
import jax.numpy as jnp
from jax import lax
from jax.experimental import pallas as pl
from jax.experimental.pallas import tpu as pltpu

D_MODEL = 2048
DEPTH = 4
CHUNK = 64
N_A_LAYERS = DEPTH // 2
RMS_EPS = 1e-6
POOL_WINDOWS = (2, 4, 8, 16)
POOL_GROUP = D_MODEL // len(POOL_WINDOWS)
MLA_HEADS = 16
NOPE_DIM = 128
ROPE_DIM = 64
V_DIM = 128
ROPE_THETA = 10000.0
MEM_HEADS = 4
MEM_HEAD_DIM = 128
MEM_INNER = MEM_HEADS * MEM_HEAD_DIM
N_GROUPS = 4
EXPERTS_PER_GROUP = 8
N_EXPERTS = N_GROUPS * EXPERTS_PER_GROUP
TOP_K = 2
EXPERT_FF = D_MODEL // 4

LANES = 128
QK_PAD = 256
HALO = 16
POOL_SUB = 256
MOE_ROWS = 256
ROW_TILE = 512
PLAN_TILE = 1024
ATT_TQ = 512
ATT_TK = 512
LOG2_E = 1.4426950408889634
VMEM_LIMIT = 56 * 1024 * 1024

F32 = jnp.float32
BF16 = jnp.bfloat16


def _params(*sem):
    return pltpu.CompilerParams(dimension_semantics=sem, vmem_limit_bytes=VMEM_LIMIT)


def _rms(x, g):
    return x * lax.rsqrt(jnp.mean(x * x, axis=-1, keepdims=True) + RMS_EPS) * g


def _dot(a, b):
    return jnp.dot(a.astype(BF16), b.astype(BF16), preferred_element_type=F32)


def _dot_t(a, b):
    return lax.dot_general(a.astype(BF16), b.astype(BF16), (((1,), (1,)), ((), ())),
                           preferred_element_type=F32)


def _pack_halves(x):
    n = x.shape[1] // 2
    lo = lax.bitcast_convert_type(x[:, :n].astype(BF16).astype(F32), jnp.uint32)
    hi = lax.bitcast_convert_type(x[:, n:].astype(BF16).astype(F32), jnp.uint32)
    return lax.shift_right_logical(lo, jnp.uint32(16)) | (hi & jnp.uint32(0xFFFF0000))


def _unpack_halves(u):
    lo = lax.bitcast_convert_type(lax.shift_left(u, jnp.uint32(16)), F32)
    hi = lax.bitcast_convert_type(u & jnp.uint32(0xFFFF0000), F32)
    return lo, hi


def _weight_spec(shape, layer=None):
    if layer is None:
        return pl.BlockSpec(shape, lambda *_: (0,) * len(shape), pipeline_mode=pl.Buffered(1))
    return pl.BlockSpec((None,) + shape, lambda *_: (layer,) + (0,) * len(shape),
                        pipeline_mode=pl.Buffered(1))


def _cast_weights_once(first, pairs):
    @pl.when(first)
    def _():
        for src, dst in pairs:
            dst[...] = src[...].astype(BF16)


def _swap_halves(x):
    lane = lax.broadcasted_iota(jnp.int32, x.shape, 1)
    return jnp.where((lane & 32) == 0, pltpu.roll(x, 96, 1), pltpu.roll(x, 32, 1))


def _rope_kernel(pos_ref, cs_ref, sn_ref):
    pos = pos_ref[...].astype(F32)
    lane = lax.broadcasted_iota(jnp.int32, (1, LANES), 1)
    half = ROPE_DIM // 2
    frac = (lane % half).astype(F32) / half
    inv = 1.0 / jnp.power(jnp.full((1, LANES), ROPE_THETA, F32), frac)
    ang = pos * inv
    sign = jnp.where((lane & half) == 0, -1.0, 1.0)
    cs_ref[...] = jnp.cos(ang)
    sn_ref[...] = jnp.sin(ang) * sign


def _rope_tables(positions):
    b, s = positions.shape
    ts = ROW_TILE
    out = jax.ShapeDtypeStruct((b, s, LANES), F32)
    spec = pl.BlockSpec((None, ts, LANES), lambda i, j: (i, j, 0))
    return pl.pallas_call(
        _rope_kernel, grid=(b, s // ts),
        in_specs=[pl.BlockSpec((None, ts, 1), lambda i, j: (i, j, 0))],
        out_specs=[spec, spec], out_shape=[out, out],
        compiler_params=_params("parallel", "parallel"), name="rope_tables",
    )(positions.reshape(b, s, 1))


def _pool_kernel(h_ref, halo_ref, g_ref, wf_ref, sc_ref, o_ref, buf_ref, w_ref):
    ts = h_ref.shape[0]
    i = pl.program_id(1)
    _cast_weights_once((pl.program_id(0) == 0) & (i == 0), [(wf_ref, w_ref)])
    g = g_ref[...]
    xn = _rms(h_ref[...], g)
    halo = _rms(halo_ref[...], g)
    buf_ref[0:HALO, :] = jnp.where(i > 0, halo, 0.0)
    buf_ref[HALO:HALO + ts, :] = xn
    r = POOL_SUB
    row = lax.broadcasted_iota(jnp.int32, (r, r + HALO), 0)
    col = lax.broadcasted_iota(jnp.int32, (r, r + HALO), 1)
    off = col - row
    for sub in range(ts // r):
        ext = buf_ref[sub * r:sub * r + r + HALO, :]
        ext_hi = ext.astype(BF16)
        ext_lo = (ext - ext_hi.astype(F32)).astype(BF16)
        t_seq = i * ts + sub * r + lax.broadcasted_iota(jnp.int32, (r, 1), 0)
        for gi, w in enumerate(POOL_WINDOWS):
            cols = slice(gi * POOL_GROUP, (gi + 1) * POOL_GROUP)
            band = jnp.where((off > HALO - w) & (off <= HALO), 1.0, 0.0).astype(BF16)
            wsum = (jnp.dot(band, ext_hi[:, cols], preferred_element_type=F32)
                    + jnp.dot(band, ext_lo[:, cols], preferred_element_type=F32))
            cnt = jnp.minimum(t_seq + 1, w).astype(F32)
            u = wsum / cnt - ext[HALO:, cols]
            y = _dot(u, w_ref[gi])
            rows = slice(sub * r, (sub + 1) * r)
            o_ref[rows, cols] = h_ref[rows, cols] + y * sc_ref[:, cols]


def _pool_layer(h, g, w_all, sc, layer):
    b, s, d = h.shape
    ts = ROW_TILE
    hpt = ts // HALO
    wshape = w_all.shape[1:]
    return pl.pallas_call(
        _pool_kernel, grid=(b, s // ts),
        in_specs=[
            pl.BlockSpec((None, ts, d), lambda i, j: (i, j, 0)),
            pl.BlockSpec((None, HALO, d), lambda i, j: (i, jnp.maximum(j * hpt - 1, 0), 0)),
            pl.BlockSpec((1, d), lambda i, j: (0, 0)),
            _weight_spec(wshape, layer),
            pl.BlockSpec((1, d), lambda i, j: (0, 0)),
        ],
        out_specs=pl.BlockSpec((None, ts, d), lambda i, j: (i, j, 0)),
        out_shape=jax.ShapeDtypeStruct(h.shape, F32),
        scratch_shapes=[pltpu.VMEM((ts + HALO, d), F32), pltpu.VMEM(wshape, BF16)],
        compiler_params=_params("arbitrary", "arbitrary"), name="pool_layer",
    )(h, h, g.reshape(1, d), w_all, sc.reshape(1, d))


def _mem_kv_kernel(m_ref, g_ref, wk_ref, wv_ref, k_ref, v_ref):
    m = _rms(m_ref[...], g_ref[...])
    k_ref[...] = _dot(m, wk_ref[...]).astype(BF16)
    v_ref[...] = _dot(m, wv_ref[...]).astype(BF16)


def _mem_kv(mem, g, wk, wv):
    b, m, d = mem.shape
    nl = g.shape[0]
    out = jax.ShapeDtypeStruct((nl, b, m, MEM_INNER), BF16)
    ospec = pl.BlockSpec((None, None, m, MEM_INNER), lambda l, i: (l, i, 0, 0))
    wspec = pl.BlockSpec((None, d, MEM_INNER), lambda l, i: (l, 0, 0))
    return pl.pallas_call(
        _mem_kv_kernel, grid=(nl, b),
        in_specs=[pl.BlockSpec((None, m, d), lambda l, i: (i, 0, 0)),
                  pl.BlockSpec((None, 1, d), lambda l, i: (l, 0, 0)), wspec, wspec],
        out_specs=[ospec, ospec], out_shape=[out, out],
        compiler_params=_params("arbitrary", "arbitrary"), name="mem_kv",
    )(mem, g.reshape(nl, 1, d), wk, wv)


def _mem_attn_kernel(h_ref, g_ref, wqf_ref, k_ref, v_ref, wof_ref, rg_ref, rw2_ref, rwh_ref, rb_ref,
                     o_ref, xn_ref, eid_ref, gate_ref, wq_ref, wo_ref):
    _cast_weights_once((pl.program_id(0) == 0) & (pl.program_id(1) == 0),
                       [(wqf_ref, wq_ref), (wof_ref, wo_ref)])
    h = h_ref[...]
    q = _dot(_rms(h, g_ref[...]), wq_ref[...]).astype(BF16)
    scale = MEM_HEAD_DIM ** -0.5
    heads = []
    for hd in range(MEM_HEADS):
        cols = slice(hd * MEM_HEAD_DIM, (hd + 1) * MEM_HEAD_DIM)
        sc = _dot_t(q[:, cols], k_ref[:, cols]) * scale
        p = jnp.exp(sc - jnp.max(sc, axis=-1, keepdims=True))
        den = jnp.sum(p, axis=-1, keepdims=True)
        heads.append(_dot(p, v_ref[:, cols]) / den)
    o = jnp.concatenate(heads, axis=-1)
    h_new = h + _dot(o, wo_ref[...])
    o_ref[...] = h_new
    _route(h_new, rg_ref, rw2_ref, rwh_ref, rb_ref, xn_ref, eid_ref, gate_ref)


def _mem_attn_route(h, g, wq, k, v, wo, layer, router_ops):
    b, s, d = h.shape
    ts = ROW_TILE
    nts = s // ts
    t = b * s
    m = k.shape[2]
    kvspec = pl.BlockSpec((None, None, m, MEM_INNER), lambda i, j: (layer, i, 0, 0))
    full = lambda a: pl.BlockSpec(a.shape, lambda i, j: (0,) * a.ndim)
    tok = lambda n: pl.BlockSpec((ts, n), lambda i, j: (i * nts + j, 0))
    row = lambda n, dt: jax.ShapeDtypeStruct((t, n), dt)
    return pl.pallas_call(
        _mem_attn_kernel, grid=(b, nts),
        in_specs=[pl.BlockSpec((None, ts, d), lambda i, j: (i, j, 0)),
                  pl.BlockSpec((1, d), lambda i, j: (0, 0)),
                  _weight_spec((d, MEM_INNER), layer),
                  kvspec, kvspec,
                  _weight_spec((MEM_INNER, d), layer)] + [full(a) for a in router_ops],
        out_specs=[pl.BlockSpec((None, ts, d), lambda i, j: (i, j, 0)),
                   tok(d // 2), tok(LANES), tok(LANES)],
        out_shape=[jax.ShapeDtypeStruct(h.shape, F32),
                   row(d // 2, jnp.uint32), row(LANES, jnp.int32), row(LANES, F32)],
        scratch_shapes=[pltpu.VMEM((d, MEM_INNER), BF16), pltpu.VMEM((MEM_INNER, d), BF16)],
        compiler_params=_params("arbitrary", "arbitrary"), name="mem_attn_route",
    )(h, g.reshape(1, d), wq, k, v, wo, *router_ops)


def _route(h, g_ref, w2_ref, wh_ref, b_ref, xn_ref, eid_ref, gate_ref):
    xn = _rms(h, g_ref[...])
    xn_ref[...] = _pack_halves(xn)
    x_hi = xn.astype(BF16)
    x_lo = (xn - x_hi.astype(F32)).astype(BF16)
    a = jnp.dot(x_hi, w2_ref[...], preferred_element_type=F32)
    c = jnp.dot(x_lo, wh_ref[...], preferred_element_type=F32)
    logits = a[:, :LANES] + a[:, LANES:] + c + b_ref[...]
    lane = lax.broadcasted_iota(jnp.int32, logits.shape, 1)
    neg = -jnp.inf
    gl = jnp.where(lane < N_GROUPS, logits, neg)
    gmax = jnp.max(gl, axis=-1, keepdims=True)
    g_sel = jnp.min(jnp.where(gl == gmax, lane, LANES), axis=-1, keepdims=True)
    g_p = 1.0 / jnp.sum(jnp.exp(gl - gmax), axis=-1, keepdims=True)
    e_idx = lane - N_GROUPS
    in_grp = (e_idx >= 0) & (e_idx < N_EXPERTS) & ((e_idx >> 3) == g_sel)
    assert EXPERTS_PER_GROUP == 8
    el = jnp.where(in_grp, logits, neg)
    m1 = jnp.max(el, axis=-1, keepdims=True)
    i1 = jnp.min(jnp.where(el == m1, lane, LANES), axis=-1, keepdims=True)
    el2 = jnp.where(lane == i1, neg, el)
    m2 = jnp.max(el2, axis=-1, keepdims=True)
    i2 = jnp.min(jnp.where(el2 == m2, lane, LANES), axis=-1, keepdims=True)
    esum = jnp.sum(jnp.exp(el - m1), axis=-1, keepdims=True)
    p1 = 1.0 / esum
    p2 = jnp.exp(m2 - m1) / esum
    w1 = g_p * p1 / (p1 + p2)
    w2 = g_p * p2 / (p1 + p2)
    eid_ref[...] = jnp.where(lane == 0, i1 - N_GROUPS, jnp.where(lane == 1, i2 - N_GROUPS, 0))
    gate_ref[...] = jnp.where(lane == 0, w1, jnp.where(lane == 1, w2, 0.0))


def _router_operands(g, rg_w, rg_b, re_w, re_b):
    d = rg_w.shape[0]
    pad = LANES - N_GROUPS - N_EXPERTS
    w = jnp.concatenate([rg_w, re_w, jnp.zeros((d, pad), F32)], axis=1)
    bias = jnp.concatenate([rg_b, re_b, jnp.zeros((pad,), F32)]).reshape(1, LANES)
    w_hi = w.astype(BF16)
    w_lo = (w - w_hi.astype(F32)).astype(BF16)
    return g.reshape(1, d), jnp.concatenate([w_hi, w_lo], axis=1), w_hi, bias


META_BLOCK_E, META_N_USED, META_FILL_START, META_FILL_COUNT = 0, 1, 2, 3
META_LANES = 2 * LANES


def _onehot(eid, k):
    lane = lax.broadcasted_iota(jnp.int32, eid.shape, 1)
    return lane == eid[:, k:k + 1]


def _plan_kernel(eid_ref, dest_ref, meta_ref, cnt_scr, carry_scr):
    ph = pl.program_id(0)
    i = pl.program_id(1)
    ts = eid_ref.shape[0]
    eid = eid_ref[...]
    o0 = _onehot(eid, 0)
    o1 = _onehot(eid, 1)
    osum = jnp.where(o0, 1.0, 0.0) + jnp.where(o1, 1.0, 0.0)
    tile_cnt = jnp.sum(osum, axis=0, keepdims=True)

    @pl.when((ph == 0) & (i == 0))
    def _():
        cnt_scr[...] = jnp.zeros(cnt_scr.shape, F32)

    @pl.when(ph == 0)
    def _():
        cnt_scr[...] = cnt_scr[...] + tile_cnt

    @pl.when(ph == 1)
    def _():
        cnt = cnt_scr[...]
        nblk = jnp.floor((cnt + (MOE_ROWS - 1)) * (1.0 / MOE_ROWS))
        er = lax.broadcasted_iota(jnp.int32, (LANES, LANES), 0)
        ec = lax.broadcasted_iota(jnp.int32, (LANES, LANES), 1)
        before = jnp.where(er < ec, 1.0, 0.0).astype(BF16)
        bstart = jnp.dot(nblk.astype(BF16), before, preferred_element_type=F32)
        row_start = bstart * MOE_ROWS

        @pl.when(i == 0)
        def _():
            carry_scr[...] = jnp.zeros(carry_scr.shape, F32)
            bend = bstart + nblk
            bend_col = jnp.sum(jnp.where(er == ec, bend[0:1], 0.0), axis=1, keepdims=True)
            e2 = lax.broadcasted_iota(jnp.int32, (LANES, META_LANES), 0)
            b2 = lax.broadcasted_iota(jnp.int32, (LANES, META_LANES), 1).astype(F32)
            done = jnp.where((bend_col <= b2) & (e2 < N_EXPERTS), 1.0, 0.0)
            block_e = jnp.minimum(jnp.sum(done, axis=0, keepdims=True), N_EXPERTS - 1.0)
            n_used = jnp.sum(nblk[0:1], axis=1, keepdims=True)
            zpad = jnp.zeros((1, META_LANES - LANES), F32)
            fill_start = jnp.concatenate([row_start[0:1] + cnt[0:1], zpad], axis=1)
            fill_count = jnp.concatenate([nblk[0:1] * MOE_ROWS - cnt[0:1], zpad], axis=1)
            row = lax.broadcasted_iota(jnp.int32, meta_ref.shape, 0)
            meta = jnp.where(row == META_BLOCK_E, block_e,
                             jnp.where(row == META_N_USED, n_used,
                                       jnp.where(row == META_FILL_START, fill_start,
                                                 jnp.where(row == META_FILL_COUNT, fill_count, 0.0))))
            meta_ref[...] = meta.astype(jnp.int32)

        rr = lax.broadcasted_iota(jnp.int32, (ts, ts), 0)
        rc = lax.broadcasted_iota(jnp.int32, (ts, ts), 1)
        earlier = jnp.where(rc < rr, 1.0, 0.0).astype(BF16)
        base = (jnp.dot(earlier, osum.astype(BF16), preferred_element_type=F32)
                + carry_scr[0:1] + row_start[0:1])
        d0 = jnp.sum(jnp.where(o0, base, 0.0), axis=1, keepdims=True)
        d1 = jnp.sum(jnp.where(o1, base + jnp.where(o0, 1.0, 0.0), 0.0), axis=1, keepdims=True)
        lane = lax.broadcasted_iota(jnp.int32, (ts, LANES), 1)
        dest_ref[...] = jnp.where(lane == 0, d0, jnp.where(lane == 1, d1, 0.0)).astype(jnp.int32)
        carry_scr[...] = carry_scr[...] + tile_cnt


def _plan(eid):
    t = eid.shape[0]
    ts = PLAN_TILE
    dest, meta = pl.pallas_call(
        _plan_kernel, grid=(2, t // ts),
        in_specs=[pl.BlockSpec((ts, LANES), lambda ph, i: (i, 0))],
        out_specs=[pl.BlockSpec((ts, LANES), lambda ph, i: (ph * i, 0)),
                   pl.BlockSpec((8, META_LANES), lambda ph, i: (0, 0))],
        out_shape=[jax.ShapeDtypeStruct((t, LANES), jnp.int32),
                   jax.ShapeDtypeStruct((8, META_LANES), jnp.int32)],
        scratch_shapes=[pltpu.VMEM((8, LANES), F32), pltpu.VMEM((8, LANES), F32)],
        compiler_params=_params("arbitrary", "arbitrary"), name="moe_plan",
    )(eid)
    dest_rows = dest[:, :TOP_K].reshape(t // ROW_TILE, 1, TOP_K * ROW_TILE)
    return dest_rows, meta


def _dispatch_kernel(meta_ref, dest_ref, x_ref, xr_hbm, zblk, zsem, sem):
    i = pl.program_id(0)
    ts = dest_ref.shape[1] // TOP_K

    @pl.when(i == 0)
    def _():
        zblk[...] = jnp.zeros(zblk.shape, zblk.dtype)
        for e in range(N_EXPERTS):
            start = meta_ref[META_FILL_START, e]
            count = meta_ref[META_FILL_COUNT, e]

            def fill(j, c, start=start):
                pltpu.make_async_copy(zblk.at[pl.ds(0, 1)], xr_hbm.at[pl.ds(start + j, 1)], zsem).start()
                return c

            def drain(j, c):
                pltpu.make_async_copy(zblk.at[pl.ds(0, 1)], xr_hbm.at[pl.ds(0, 1)], zsem).wait()
                return c

            lax.fori_loop(0, count, fill, 0)
            lax.fori_loop(0, count, drain, 0)

        n_used = meta_ref[META_N_USED, 0]
        n_blocks = xr_hbm.shape[0] // MOE_ROWS

        def tail(b, c):
            cp = pltpu.make_async_copy(zblk, xr_hbm.at[pl.ds(pl.multiple_of(b * MOE_ROWS, MOE_ROWS), MOE_ROWS)], zsem)
            cp.start()
            cp.wait()
            return c

        lax.fori_loop(n_used, n_blocks, tail, 0)

    for t in range(ts):
        for k in range(TOP_K):
            pltpu.make_async_copy(x_ref.at[pl.ds(t, 1)],
                                  xr_hbm.at[pl.ds(dest_ref[0, TOP_K * t + k], 1)], sem).start()
    for _ in range(ts * TOP_K):
        pltpu.make_async_copy(x_ref.at[pl.ds(0, 1)], xr_hbm.at[pl.ds(0, 1)], sem).wait()


def _dispatch(xn, dest_rows, meta, n_rows):
    t, d = xn.shape
    nt = dest_rows.shape[0]
    grid_spec = pltpu.PrefetchScalarGridSpec(
        num_scalar_prefetch=1, grid=(nt,),
        in_specs=[pl.BlockSpec((None, 1, dest_rows.shape[2]), lambda i, m: (i, 0, 0),
                               memory_space=pltpu.SMEM),
                  pl.BlockSpec((dest_rows.shape[2] // TOP_K, d), lambda i, m: (i, 0))],
        out_specs=pl.BlockSpec(memory_space=pl.ANY),
        scratch_shapes=[pltpu.VMEM((MOE_ROWS, d), xn.dtype),
                        pltpu.SemaphoreType.DMA(()), pltpu.SemaphoreType.DMA(())])
    return pl.pallas_call(
        _dispatch_kernel, grid_spec=grid_spec,
        out_shape=jax.ShapeDtypeStruct((n_rows, d), xn.dtype),
        compiler_params=_params("arbitrary"), name="moe_dispatch",
    )(meta, dest_rows, xn)


def _expert_kernel(meta_ref, x_ref, wg_hbm, wu_hbm, wd_hbm, y_ref, wg_f32, wu_f32, wd_f32,
                   wg_ref, wu_ref, wd_ref, wsem, seq_ref, *, layer):
    blk = pl.program_id(0)
    n_used = meta_ref[META_N_USED, 0]
    active = blk < n_used
    e_cur = meta_ref[META_BLOCK_E, blk]
    new_expert = (blk == 0) | (e_cur != meta_ref[META_BLOCK_E, jnp.maximum(blk - 1, 0)])
    streams = ((wg_hbm, wg_f32, wg_ref), (wu_hbm, wu_f32, wu_ref), (wd_hbm, wd_f32, wd_ref))

    def weight_copies(e, slot):
        return [pltpu.make_async_copy(src.at[layer, e], stage.at[slot], wsem.at[slot, j])
                for j, (src, stage, _) in enumerate(streams)]

    @pl.when(active & (blk == 0))
    def _():
        seq_ref[0] = 0
        for cp in weight_copies(e_cur, 0):
            cp.start()

    @pl.when(active & new_expert)
    def _():
        slot = seq_ref[0] % 2
        nxt = lax.while_loop(lambda j: (j < n_used) & (meta_ref[META_BLOCK_E, jnp.minimum(j, n_used - 1)] == e_cur),
                             lambda j: j + 1, blk + 1)

        @pl.when(nxt < n_used)
        def _():
            for cp in weight_copies(meta_ref[META_BLOCK_E, nxt], 1 - slot):
                cp.start()

        for cp, (_, stage, dst) in zip(weight_copies(e_cur, slot), streams):
            cp.wait()
            dst[...] = stage[slot].astype(BF16)
        seq_ref[0] = seq_ref[0] + 1

    @pl.when(active)
    def _():
        half = x_ref.shape[1]
        x_lo, x_hi = _unpack_halves(x_ref[...])
        x_lo = x_lo.astype(BF16)
        x_hi = x_hi.astype(BF16)

        def proj(w_ref):
            return (jnp.dot(x_lo, w_ref[0:half, :], preferred_element_type=F32)
                    + jnp.dot(x_hi, w_ref[half:, :], preferred_element_type=F32))

        gate = proj(wg_ref)
        mid = (gate * jax.nn.sigmoid(gate)) * proj(wu_ref)
        y = jnp.dot(mid.astype(BF16), wd_ref[...], preferred_element_type=F32)
        y_ref[...] = _pack_halves(y)

    @pl.when(blk >= meta_ref[META_N_USED, 0])
    def _():
        y_ref[...] = jnp.zeros(y_ref.shape, y_ref.dtype)


def _experts(x_rows, meta, wg, wu, wd, layer):
    n_rows, half = x_rows.shape
    d, ff = wg.shape[-2:]
    assert 2 * half == d
    anyspec = pl.BlockSpec(memory_space=pl.ANY)
    wshapes = [(d, ff), (d, ff), (ff, d)]
    grid_spec = pltpu.PrefetchScalarGridSpec(
        num_scalar_prefetch=1, grid=(n_rows // MOE_ROWS,),
        in_specs=[pl.BlockSpec((MOE_ROWS, half), lambda i, m: (jnp.minimum(i, m[META_N_USED, 0] - 1), 0)),
                  anyspec, anyspec, anyspec],
        out_specs=pl.BlockSpec((MOE_ROWS, half), lambda i, m: (i, 0)),
        scratch_shapes=([pltpu.VMEM((2,) + ws, F32) for ws in wshapes]
                        + [pltpu.VMEM(ws, BF16) for ws in wshapes]
                        + [pltpu.SemaphoreType.DMA((2, len(wshapes))), pltpu.SMEM((1,), jnp.int32)]))
    return pl.pallas_call(
        functools.partial(_expert_kernel, layer=layer), grid_spec=grid_spec,
        out_shape=jax.ShapeDtypeStruct((n_rows, half), jnp.uint32),
        compiler_params=_params("arbitrary"), name="experts",
    )(meta, x_rows, wg, wu, wd)


def _combine_kernel(dest_ref, dnext_ref, h_ref, gate_ref, g_ref, y_hbm, o_ref, ybuf, sem, *, final):
    i = pl.program_id(0)
    nt = pl.num_programs(0)
    ts = h_ref.shape[0]
    slot = i % 2

    def gather(rows_ref, dst_slot):
        def fetch(t, c):
            for k in range(TOP_K):
                pltpu.make_async_copy(y_hbm.at[pl.ds(rows_ref[0, TOP_K * t + k], 1)],
                                      ybuf.at[dst_slot, k, pl.ds(t, 1)], sem.at[dst_slot]).start()
            return c
        lax.fori_loop(0, ts, fetch, 0, unroll=8)

    @pl.when(i == 0)
    def _():
        gather(dest_ref, 0)

    for par in range(2):
        @pl.when((i + 1 < nt) & (slot == par))
        def _(par=par):
            for t in range(ts):
                for k in range(TOP_K):
                    pltpu.make_async_copy(y_hbm.at[pl.ds(dnext_ref[0, TOP_K * t + k], 1)],
                                          ybuf.at[1 - par, k, pl.ds(t, 1)], sem.at[1 - par]).start()

    def done(t, c):
        for k in range(TOP_K):
            pltpu.make_async_copy(y_hbm.at[pl.ds(0, 1)], ybuf.at[slot, k, pl.ds(0, 1)], sem.at[slot]).wait()
        return c

    lax.fori_loop(0, ts, done, 0, unroll=8)
    gate = gate_ref[...]
    y0_lo, y0_hi = _unpack_halves(ybuf[slot, 0])
    y1_lo, y1_hi = _unpack_halves(ybuf[slot, 1])
    moe = jnp.concatenate([gate[:, 0:1] * y0_lo + gate[:, 1:2] * y1_lo,
                           gate[:, 0:1] * y0_hi + gate[:, 1:2] * y1_hi], axis=1)
    out = h_ref[...] + moe
    o_ref[...] = _rms(out, g_ref[...]) if final else out


def _combine(h2, y_rows, dest_rows, gate, g, final):
    t, d = h2.shape
    nt = dest_rows.shape[0]
    ts = t // nt
    assert TOP_K == 2
    return pl.pallas_call(
        functools.partial(_combine_kernel, final=final), grid=(nt,),
        in_specs=[pl.BlockSpec((None, 1, TOP_K * ts), lambda i: (i, 0, 0), memory_space=pltpu.SMEM),
                  pl.BlockSpec((None, 1, TOP_K * ts), lambda i: (jnp.minimum(i + 1, nt - 1), 0, 0),
                               memory_space=pltpu.SMEM),
                  pl.BlockSpec((ts, d), lambda i: (i, 0)),
                  pl.BlockSpec((ts, LANES), lambda i: (i, 0)),
                  pl.BlockSpec((1, d), lambda i: (0, 0)),
                  pl.BlockSpec(memory_space=pl.ANY)],
        out_specs=pl.BlockSpec((ts, d), lambda i: (i, 0)),
        out_shape=jax.ShapeDtypeStruct(h2.shape, F32),
        scratch_shapes=[pltpu.VMEM((2, TOP_K, ts, d // 2), y_rows.dtype), pltpu.SemaphoreType.DMA((2,))],
        compiler_params=_params("arbitrary"), name="moe_combine",
    )(dest_rows, dest_rows, h2, gate, g.reshape(1, d), y_rows)


def _moe(h, xn, eid, gate, wg, wu, wd, layer, final_g, final):
    b, s, d = h.shape
    t = b * s
    h2 = h.reshape(t, d)
    dest_rows, meta = _plan(eid)
    n_rows = -(-(t * TOP_K + N_EXPERTS * (MOE_ROWS - 1)) // MOE_ROWS) * MOE_ROWS
    assert n_rows // MOE_ROWS <= META_LANES
    x_rows = _dispatch(xn, dest_rows, meta, n_rows)
    y_rows = _experts(x_rows, meta, wg, wu, wd, layer)
    return _combine(h2, y_rows, dest_rows, gate, final_g, final).reshape(b, s, d)


def _kv_kernel(h_ref, gin_ref, wdkvf_ref, glat_ref, wkrf_ref, wukf_ref, wuvf_ref, cs_ref, sn_ref,
               k_ref, v_ref, wdkv_ref, wkr_ref, wuk_ref, wuv_ref):
    _cast_weights_once((pl.program_id(0) == 0) & (pl.program_id(1) == 0),
                       [(wdkvf_ref, wdkv_ref), (wkrf_ref, wkr_ref), (wukf_ref, wuk_ref),
                        (wuvf_ref, wuv_ref)])
    hn = _rms(h_ref[...], gin_ref[...])
    ckv = _rms(_dot(hn, wdkv_ref[...]), glat_ref[...])
    kr = _dot(hn, wkr_ref[...])
    kr = kr * cs_ref[...] + _swap_halves(kr) * sn_ref[...]
    lane = lax.broadcasted_iota(jnp.int32, kr.shape, 1)
    kr = jnp.where(lane < ROPE_DIM, kr, 0.0).astype(BF16)
    kn = _dot(ckv, wuk_ref[...]).astype(BF16)
    vv = _dot(ckv, wuv_ref[...])
    for hd in range(MLA_HEADS):
        k_ref[hd, :, 0:NOPE_DIM] = kn[:, hd * NOPE_DIM:(hd + 1) * NOPE_DIM]
        k_ref[hd, :, NOPE_DIM:QK_PAD] = kr
        v_ref[hd] = vv[:, hd * V_DIM:(hd + 1) * V_DIM].T.astype(BF16)


def _shared_kv(h, cs, sn, gin, wdkv, glat, wkr, wuk, wuv):
    b, s, d = h.shape
    ts = ROW_TILE
    r = wdkv.shape[1]
    wkr_p = jnp.concatenate([wkr, jnp.zeros((d, LANES - ROPE_DIM), F32)], axis=1)
    full = lambda shape: pl.BlockSpec(shape, lambda i, j: (0,) * len(shape))
    tab = pl.BlockSpec((None, ts, LANES), lambda i, j: (i, j, 0))
    wshapes = [(d, r), (d, LANES), (r, MLA_HEADS * NOPE_DIM), (r, MLA_HEADS * V_DIM)]
    return pl.pallas_call(
        _kv_kernel, grid=(b, s // ts),
        in_specs=[pl.BlockSpec((None, ts, d), lambda i, j: (i, j, 0)),
                  full((1, d)), _weight_spec(wshapes[0]), full((1, r)), _weight_spec(wshapes[1]),
                  _weight_spec(wshapes[2]), _weight_spec(wshapes[3]), tab, tab],
        out_specs=[pl.BlockSpec((None, MLA_HEADS, ts, QK_PAD), lambda i, j: (i, 0, j, 0)),
                   pl.BlockSpec((None, MLA_HEADS, V_DIM, ts), lambda i, j: (i, 0, 0, j))],
        out_shape=[jax.ShapeDtypeStruct((b, MLA_HEADS, s, QK_PAD), BF16),
                   jax.ShapeDtypeStruct((b, MLA_HEADS, V_DIM, s), BF16)],
        scratch_shapes=[pltpu.VMEM(ws, BF16) for ws in wshapes],
        compiler_params=_params("arbitrary", "arbitrary"), name="mla_shared_kv",
    )(h, gin.reshape(1, d), wdkv, glat.reshape(1, r), wkr_p, wuk, wuv, cs, sn)


def _q_kernel(h_ref, g_ref, wdqf_ref, glat_ref, wuqf_ref, wqrf_ref, cs_ref, sn_ref, q_ref,
              wdq_ref, wuq_ref, wqr_ref):
    _cast_weights_once((pl.program_id(0) == 0) & (pl.program_id(1) == 0),
                       [(wdqf_ref, wdq_ref), (wuqf_ref, wuq_ref), (wqrf_ref, wqr_ref)])
    xn = _rms(h_ref[...], g_ref[...])
    cq = _rms(_dot(xn, wdq_ref[...]), glat_ref[...])
    scale = (NOPE_DIM + ROPE_DIM) ** -0.5 * LOG2_E
    qn = (_dot(cq, wuq_ref[...]) * scale).astype(BF16)
    qr = _dot(cq, wqr_ref[...]) * scale
    cs = cs_ref[...]
    sn = sn_ref[...]
    lane = lax.broadcasted_iota(jnp.int32, cs.shape, 1)
    for pair in range(MLA_HEADS // 2):
        x = qr[:, pair * LANES:(pair + 1) * LANES]
        rot = x * cs + _swap_halves(x) * sn
        for odd in range(2):
            hd = 2 * pair + odd
            part = pltpu.roll(rot, ROPE_DIM, 1) if odd else rot
            q_ref[hd, :, 0:NOPE_DIM] = qn[:, hd * NOPE_DIM:(hd + 1) * NOPE_DIM]
            q_ref[hd, :, NOPE_DIM:QK_PAD] = jnp.where(lane < ROPE_DIM, part, 0.0).astype(BF16)


def _q_proj(h, cs, sn, g, wdq, glat, wuq, wqr, layer):
    b, s, d = h.shape
    ts = ROW_TILE
    r = wdq.shape[-1]
    full = lambda shape: pl.BlockSpec(shape, lambda i, j: (0,) * len(shape))
    tab = pl.BlockSpec((None, ts, LANES), lambda i, j: (i, j, 0))
    wshapes = [(d, r), (r, MLA_HEADS * NOPE_DIM), (r, MLA_HEADS * ROPE_DIM)]
    return pl.pallas_call(
        _q_kernel, grid=(b, s // ts),
        in_specs=[pl.BlockSpec((None, ts, d), lambda i, j: (i, j, 0)),
                  full((1, d)), _weight_spec(wshapes[0], layer), full((1, r)),
                  _weight_spec(wshapes[1], layer), _weight_spec(wshapes[2], layer), tab, tab],
        out_specs=pl.BlockSpec((None, MLA_HEADS, ts, QK_PAD), lambda i, j: (i, 0, j, 0)),
        out_shape=jax.ShapeDtypeStruct((b, MLA_HEADS, s, QK_PAD), BF16),
        scratch_shapes=[pltpu.VMEM(ws, BF16) for ws in wshapes],
        compiler_params=_params("arbitrary", "arbitrary"), name="mla_q_proj",
    )(h, g.reshape(1, d), wdq, glat.reshape(1, r), wuq, wqr, cs, sn)


def _flash_kernel(qt_ref, kt_ref, q_ref, k_ref, v_ref, o_ref, m_scr, l_scr, acc_scr, s_scr):
    qi = qt_ref[pl.program_id(1)]
    ki = kt_ref[pl.program_id(1)]
    tq = q_ref.shape[1]
    tk = k_ref.shape[1]

    @pl.when(ki == 0)
    def _():
        m_scr[...] = jnp.full(m_scr.shape, -jnp.inf, F32)
        l_scr[...] = jnp.zeros(l_scr.shape, F32)
        acc_scr[...] = jnp.zeros(acc_scr.shape, F32)

    def scores(hd):
        s_scr[hd % 2] = lax.dot_general(k_ref[hd], q_ref[hd], (((1,), (1,)), ((), ())),
                                        preferred_element_type=F32)

    def softmax_pv(hd, bias):
        s = s_scr[hd % 2]
        if bias is not None:
            s = s + bias
        m_old = m_scr[hd]
        m_new = jnp.maximum(m_old, jnp.max(s, axis=0, keepdims=True))
        alpha = jnp.exp2(m_old - m_new)
        p = jnp.exp2(s - m_new[0:1])
        l_scr[hd] = alpha * l_scr[hd] + jnp.sum(p, axis=0, keepdims=True)
        acc_scr[hd] = alpha[0:1] * acc_scr[hd] + jnp.dot(v_ref[hd], p.astype(BF16),
                                                         preferred_element_type=F32)
        m_scr[hd] = m_new

    def all_heads(bias):
        scores(0)
        for hd in range(MLA_HEADS):
            if hd + 1 < MLA_HEADS:
                scores(hd + 1)
            softmax_pv(hd, bias)

    @pl.when(ki < qi)
    def _():
        all_heads(None)

    @pl.when(ki == qi)
    def _():
        keyc = lax.broadcasted_iota(jnp.int32, (tk, tq), 0) // CHUNK
        qryc = lax.broadcasted_iota(jnp.int32, (tk, tq), 1) // CHUNK
        all_heads(jnp.where(keyc <= qryc, 0.0, -jnp.inf))
        for hd in range(MLA_HEADS):
            o_t = acc_scr[hd] / l_scr[hd][0:1]
            o_ref[:, hd * V_DIM:(hd + 1) * V_DIM] = o_t.T.astype(BF16)


def _flash(q, k, v):
    b, nh, s, _ = q.shape
    tq, tk = ATT_TQ, ATT_TK
    assert tq == tk and tq % CHUNK == 0
    pairs = [(qi, ki) for qi in range(s // tq) for ki in range(qi + 1)]
    q_tab = jnp.array([p[0] for p in pairs], jnp.int32)
    k_tab = jnp.array([p[1] for p in pairs], jnp.int32)
    grid_spec = pltpu.PrefetchScalarGridSpec(
        num_scalar_prefetch=2, grid=(b, len(pairs)),
        in_specs=[pl.BlockSpec((None, nh, tq, QK_PAD), lambda i, j, qt, kt: (i, 0, qt[j], 0)),
                  pl.BlockSpec((None, nh, tk, QK_PAD), lambda i, j, qt, kt: (i, 0, kt[j], 0)),
                  pl.BlockSpec((None, nh, V_DIM, tk), lambda i, j, qt, kt: (i, 0, 0, kt[j]))],
        out_specs=pl.BlockSpec((None, tq, nh * V_DIM), lambda i, j, qt, kt: (i, qt[j], 0)),
        scratch_shapes=[pltpu.VMEM((nh, 8, tq), F32), pltpu.VMEM((nh, 8, tq), F32),
                        pltpu.VMEM((nh, V_DIM, tq), F32), pltpu.VMEM((2, tk, tq), F32)])
    return pl.pallas_call(
        _flash_kernel, grid_spec=grid_spec,
        out_shape=jax.ShapeDtypeStruct((b, s, nh * V_DIM), BF16),
        compiler_params=_params("arbitrary", "arbitrary"), name="mla_flash",
    )(q_tab, k_tab, q, k, v)


def _oproj_kernel(h_ref, o_ref, wf_ref, out_ref, w_ref):
    _cast_weights_once((pl.program_id(0) == 0) & (pl.program_id(1) == 0), [(wf_ref, w_ref)])
    out_ref[...] = h_ref[...] + jnp.dot(o_ref[...], w_ref[...], preferred_element_type=F32)


def _o_proj(h, o, w_all, layer):
    b, s, d = h.shape
    ts = ROW_TILE
    tile = lambda n: pl.BlockSpec((None, ts, n), lambda i, j: (i, j, 0))
    wshape = w_all.shape[1:]
    return pl.pallas_call(
        _oproj_kernel, grid=(b, s // ts),
        in_specs=[tile(d), tile(o.shape[-1]), _weight_spec(wshape, layer)],
        out_specs=tile(d), out_shape=jax.ShapeDtypeStruct(h.shape, F32),
        scratch_shapes=[pltpu.VMEM(wshape, BF16)],
        compiler_params=_params("arbitrary", "arbitrary"), name="mla_o_proj",
    )(h, o, w_all)


def kernel(x, mem, positions, norm_mix, norm_mem, norm_memtok, norm_ffn, pool_w, pool_scale, kv_in_norm, w_dkv, kv_latent_norm, w_kr, w_uk, w_uv, w_dq, q_latent_norm, w_uq, w_qr, w_o, mem_wq, mem_wk, mem_wv, mem_wo, rg_w, rg_b, re_w, re_b, w_gate, w_up, w_down, final_norm):
    cs, sn = _rope_tables(positions)
    mk, mv = _mem_kv(mem, norm_memtok, mem_wk, mem_wv)
    h = x
    k_sh = v_sh = None
    for l in range(DEPTH):
        if l < N_A_LAYERS:
            h = _pool_layer(h, norm_mix[l], pool_w, pool_scale[l], l)
        else:
            if l == N_A_LAYERS:
                k_sh, v_sh = _shared_kv(h, cs, sn, kv_in_norm, w_dkv, kv_latent_norm, w_kr, w_uk, w_uv)
            j = l - N_A_LAYERS
            q = _q_proj(h, cs, sn, norm_mix[l], w_dq, q_latent_norm[j], w_uq, w_qr, j)
            h = _o_proj(h, _flash(q, k_sh, v_sh), w_o, j)
        router_ops = _router_operands(norm_ffn[l], rg_w[l], rg_b[l], re_w[l], re_b[l])
        h, xn, eid, gate = _mem_attn_route(h, norm_mem[l], mem_wq, mk, mv, mem_wo, l, router_ops)
        h = _moe(h, xn, eid, gate, w_gate, w_up, w_down, l, final_norm, l == DEPTH - 1)
    return h
```
